```python
import math
import jax, jax.numpy as jnp
from jax import lax
import numpy as np

D_MODEL = 1024
BATCH = 4
SEQ = 8192
DEPTH = 2
DEC_BATCH = 16
DEC_SEQ = 16
PAST_LEN = 4096

CHUNK = 64
N_META = 16
N_MIXERS = 2
N_LAYERS_A = (DEPTH + N_MIXERS - 1) // N_MIXERS
N_LAYERS_B = DEPTH // N_MIXERS
HG_HEADS = 8
HG_DK = 128
HG_DV = D_MODEL // HG_HEADS
HG_DIM = HG_HEADS * HG_DK
HG_VDIM = HG_HEADS * HG_DV
GLA_BLOCK = 16
D_RNN = D_MODEL
RG_BLOCKS = 8
RG_BW = D_RNN // RG_BLOCKS
CONV_W = 4
RG_C = 8.0
D_FF = 2816
FFN_RES = 0.5
EPS = 1e-6

kernel_name = 'hgrn2_rglru_macaron_stream_step'


def rmsnorm(x, g):
    xf = x.astype(jnp.float32)
    y = xf * lax.rsqrt(jnp.mean(xf * xf, axis=-1, keepdims=True) + EPS)
    return (y * g.astype(jnp.float32)).astype(x.dtype)


def swiglu_half(x, g, w_in, w_out):
    a, b = jnp.split(rmsnorm(x, g) @ w_in, 2, axis=-1)
    return (jax.nn.silu(a) * b) @ w_out


def gla_blocked(q, k, v, logf, s0, block):
    B, T, H, DK = q.shape
    DV = v.shape[-1]
    n = T // block

    def to_blocks(t):
        return t.reshape(B, n, block, H, t.shape[-1]).transpose(1, 0, 3, 2, 4)

    mask = jnp.tril(jnp.ones((block, block), dtype=bool))

    def step(S, inp):
        qc, kc, vc, gc = inp
        b = jnp.cumsum(gc, axis=-2)
        b_last = b[:, :, -1:, :]
        q_d = qc * jnp.exp(b)
        k_d = kc * jnp.exp(-b)
        scores = jnp.where(mask, jnp.einsum('bhtk,bhsk->bhts', q_d, k_d), 0.0)
        o = (jnp.einsum('bhts,bhsv->bhtv', scores, vc)
             + jnp.einsum('bhtk,bhkv->bhtv', q_d, S))
        S_new = (jnp.exp(b_last[:, :, 0, :])[..., None] * S
                 + jnp.einsum('bhsk,bhsv->bhkv', kc * jnp.exp(b_last - b), vc))
        return S_new, o

    S, o = lax.scan(step, s0.astype(jnp.float32),
                    (to_blocks(q), to_blocks(k), to_blocks(v), to_blocks(logf)))
    o = o.transpose(1, 0, 3, 2, 4).reshape(B, T, H, DV)
    return o, S


def hgrn2_mixer(x, s0, w_in, lb, onorm, w_out):
    B, T, _ = x.shape
    proj = (x @ w_in).astype(jnp.float32)
    q, fr, i, g = jnp.split(proj, [HG_DIM, 2 * HG_DIM, 2 * HG_DIM + HG_VDIM], axis=-1)
    f = lb + (1.0 - lb) * jax.nn.sigmoid(fr)
    logf = jnp.log(f)
    k = 1.0 - f
    q = jax.nn.silu(q)
    o, S = gla_blocked(q.reshape(B, T, HG_HEADS, HG_DK), k.reshape(B, T, HG_HEADS, HG_DK),
                       i.reshape(B, T, HG_HEADS, HG_DV), logf.reshape(B, T, HG_HEADS, HG_DK),
                       s0, math.gcd(T, GLA_BLOCK))
    o = o * lax.rsqrt(jnp.mean(o * o, axis=-1, keepdims=True) + EPS)
    o = o * onorm.astype(jnp.float32).reshape(HG_HEADS, HG_DV)
    o = o.reshape(B, T, HG_VDIM) * jax.nn.silu(g)
    return o.astype(x.dtype) @ w_out, S


def _lin_combine(left, right):
    a1, b1 = left
    a2, b2 = right
    return a1 * a2, a2 * b1 + b2


def rglru_mixer(x, h0, conv0, w_in, conv_w, conv_b, wa, ba, wx, bx, lam, w_out):
    B, T, _ = x.shape
    xb, gb = jnp.split(x @ w_in, 2, axis=-1)
    xpad = jnp.concatenate([conv0.astype(xb.dtype), xb], axis=1)
    conv = conv_b + xpad[:, 0:T] * conv_w[0]
    for j in range(1, CONV_W):
        conv = conv + xpad[:, j:j + T] * conv_w[j]
    new_conv = xpad[:, T:]
    cf = conv.astype(jnp.float32)
    cb = cf.reshape(B, T, RG_BLOCKS, RG_BW)
    r = jax.nn.sigmoid(jnp.einsum('btnc,ncd->btnd', cb, wa.astype(jnp.float32)).reshape(B, T, D_RNN) + ba)
    ig = jax.nn.sigmoid(jnp.einsum('btnc,ncd->btnd', cb, wx.astype(jnp.float32)).reshape(B, T, D_RNN) + bx)
    log_a = -RG_C * r * jax.nn.softplus(-lam.astype(jnp.float32))
    a = jnp.exp(log_a)
    u = jnp.sqrt(-jnp.expm1(2.0 * log_a)) * (ig * cf)
    a_cum, b_cum = lax.associative_scan(_lin_combine, (a, u), axis=1)
    h = a_cum * h0.astype(jnp.float32)[:, None, :] + b_cum
    y = (h * jax.nn.gelu(gb.astype(jnp.float32))).astype(x.dtype)
    return y @ w_out, h[:, -1], new_conv


def run_trunk(h, s_hgrn, s_h, s_conv, ffn_norm, ffn_w_in, ffn_w_out, mix_norm,
              a_w_in, a_lb, a_onorm, a_w_out, b_w_in, b_conv_w, b_conv_b,
              b_wa, b_ba, b_wx, b_bx, b_lambda, b_w_out):
    lb_all = jnp.cumsum(jax.nn.softmax(a_lb.astype(jnp.float32), axis=0), axis=0)
    new_S, new_h, new_c = [], [], []
    for layer in range(DEPTH):
        h = h + FFN_RES * swiglu_half(h, ffn_norm[layer, 0], ffn_w_in[layer, 0], ffn_w_out[layer, 0])
        hn = rmsnorm(h, mix_norm[layer])
        j = layer // N_MIXERS
        if layer % N_MIXERS == 0:
            out, S = hgrn2_mixer(hn, s_hgrn[j], a_w_in[j], lb_all[j], a_onorm[j], a_w_out[j])
            new_S.append(S)
        else:
            out, hl, cv = rglru_mixer(hn, s_h[j], s_conv[j], b_w_in[j], b_conv_w[j], b_conv_b[j],
                                      b_wa[j], b_ba[j], b_wx[j], b_bx[j], b_lambda[j], b_w_out[j])
            new_h.append(hl)
            new_c.append(cv)
        h = h + out
        h = h + FFN_RES * swiglu_half(h, ffn_norm[layer, 1], ffn_w_in[layer, 1], ffn_w_out[layer, 1])
    return h, jnp.stack(new_S), jnp.stack(new_h), jnp.stack(new_c)


def setup_inputs(seed: int = 0) -> dict:
    key = jax.random.key(seed)
    ks = jax.random.split(key, 24)
    f32 = jnp.float32

    def nrm(k, shape, scale):
        return jax.random.normal(k, shape, f32) * scale

    u = jax.random.uniform(ks[21], (N_LAYERS_B, D_RNN), f32, 0.9, 0.999)
    s = u ** (1.0 / RG_C)
    lam = jnp.log(s) - jnp.log1p(-s)
    return {
        'x_prompt': nrm(ks[0], (BATCH, SEQ, D_MODEL), 1.0),
        'x_sample': nrm(ks[1], (DEC_BATCH, DEC_SEQ, D_MODEL), 1.0),
        'state_hgrn': nrm(ks[2], (N_LAYERS_A, DEC_BATCH, HG_HEADS, HG_DK, HG_DV), 0.5),
        'state_rglru': nrm(ks[3], (N_LAYERS_B, DEC_BATCH, D_RNN), 0.5),
        'state_conv': nrm(ks[4], (N_LAYERS_B, DEC_BATCH, CONV_W - 1, D_RNN), 1.0),
        'meta_tokens': nrm(ks[5], (N_META, D_MODEL), 1.0),
        'ffn_norm': 1.0 + nrm(ks[6], (DEPTH, 2, D_MODEL), 0.05),
        'ffn_w_in': nrm(ks[7], (DEPTH, 2, D_MODEL, 2 * D_FF), D_MODEL ** -0.5),
        'ffn_w_out': nrm(ks[8], (DEPTH, 2, D_FF, D_MODEL), D_FF ** -0.5),
        'mix_norm': 1.0 + nrm(ks[9], (DEPTH, D_MODEL), 0.05),
        'a_w_in': nrm(ks[10], (N_LAYERS_A, D_MODEL, 2 * HG_DIM + 2 * HG_VDIM), D_MODEL ** -0.5),
        'a_lb': nrm(ks[11], (N_LAYERS_A + 1, HG_DIM), 0.5),
        'a_onorm': 1.0 + nrm(ks[12], (N_LAYERS_A, HG_VDIM), 0.05),
        'a_w_out': nrm(ks[13], (N_LAYERS_A, HG_VDIM, D_MODEL), HG_VDIM ** -0.5),
        'b_w_in': nrm(ks[14], (N_LAYERS_B, D_MODEL, 2 * D_RNN), D_MODEL ** -0.5),
        'b_conv_w': nrm(ks[15], (N_LAYERS_B, CONV_W, D_RNN), CONV_W ** -0.5),
        'b_conv_b': nrm(ks[16], (N_LAYERS_B, D_RNN), 0.02),
        'b_wa': nrm(ks[17], (N_LAYERS_B, RG_BLOCKS, RG_BW, RG_BW), RG_BW ** -0.5),
        'b_ba': nrm(ks[18], (N_LAYERS_B, D_RNN), 0.02),
        'b_wx': nrm(ks[19], (N_LAYERS_B, RG_BLOCKS, RG_BW, RG_BW), RG_BW ** -0.5),
        'b_bx': nrm(ks[20], (N_LAYERS_B, D_RNN), 0.02),
        'b_lambda': lam,
        'b_w_out': nrm(ks[22], (N_LAYERS_B, D_RNN, D_MODEL), D_RNN ** -0.5),
        'final_norm': 1.0 + nrm(ks[23], (D_MODEL,), 0.05),
    }


def reference(x_prompt, x_sample, state_hgrn, state_rglru, state_conv, meta_tokens,
              ffn_norm, ffn_w_in, ffn_w_out, mix_norm, a_w_in, a_lb, a_onorm, a_w_out,
              b_w_in, b_conv_w, b_conv_b, b_wa, b_ba, b_wx, b_bx, b_lambda, b_w_out, final_norm):
    dt = x_prompt.dtype
    B = x_prompt.shape[0]
    meta = jnp.broadcast_to(meta_tokens.astype(dt)[None], (B, N_META, D_MODEL))
    xp = jnp.concatenate([meta, x_prompt], axis=1)
    zS = jnp.zeros((N_LAYERS_A, B, HG_HEADS, HG_DK, HG_DV), jnp.float32)
    zh = jnp.zeros((N_LAYERS_B, B, D_RNN), jnp.float32)
    zc = jnp.zeros((N_LAYERS_B, B, CONV_W - 1, D_RNN), dt)
    hp, hgrn_p, rglru_p, conv_p = run_trunk(
        xp, zS, zh, zc, ffn_norm, ffn_w_in, ffn_w_out, mix_norm, a_w_in, a_lb, a_onorm, a_w_out,
        b_w_in, b_conv_w, b_conv_b, b_wa, b_ba, b_wx, b_bx, b_lambda, b_w_out)
    y_prompt = rmsnorm(hp[:, N_META:], final_norm)
    hs, hgrn_s, rglru_s, conv_s = run_trunk(
        x_sample, state_hgrn, state_rglru, state_conv, ffn_norm, ffn_w_in, ffn_w_out, mix_norm,
        a_w_in, a_lb, a_onorm, a_w_out, b_w_in, b_conv_w, b_conv_b, b_wa, b_ba, b_wx, b_bx,
        b_lambda, b_w_out)
    y_sample = rmsnorm(hs, final_norm)
    return (y_prompt, y_sample, hgrn_p, hgrn_s, rglru_p, rglru_s, conv_p, conv_s)
```

```python
import functools

import jax
import jax.numpy as jnp
from jax import lax
from jax.experimental import pallas as pl
from jax.experimental.pallas import tpu as pltpu

F32 = jnp.float32
BF16 = jnp.bfloat16

EPS = 1e-6
FFN_RES = 0.5
RG_C = 8.0
N_MIXERS = 2
HG_HEADS = 8
RG_BLOCKS = 8
CONV_W = 4

V7X_VMEM_BYTES = 64 * 1024 * 1024
VMEM_LIMIT_BYTES = V7X_VMEM_BYTES - 8 * 1024 * 1024
SUBLANES = 8


def _rms(x, g):
    return x * lax.rsqrt(jnp.mean(x * x, axis=-1, keepdims=True) + EPS) * g


def _dot(a, b):
    return jnp.dot(a, b, preferred_element_type=F32)


def _dot_nt(a, b):
    return lax.dot_general(a, b, (((1,), (1,)), ((), ())), preferred_element_type=F32)


def _dot_tn(a, b):
    return lax.dot_general(a, b, (((0,), (0,)), ((), ())), preferred_element_type=F32)


def _const_spec(shape):
    zeros = (0,) * len(shape)
    return pl.BlockSpec(shape, lambda *_: zeros)


def _ffn_kernel(x_ref, g_ref, win_ref, wout_ref, fg_ref, o_ref, *, d_ff, n_chunks, final_norm):
    x = x_ref[...]
    yb = _rms(x, g_ref[...]).astype(BF16)
    fc = d_ff // n_chunks
    acc = jnp.zeros(x.shape, F32)
    for c in range(n_chunks):
        a = _dot(yb, win_ref[:, c * fc:(c + 1) * fc])
        b = _dot(yb, win_ref[:, d_ff + c * fc:d_ff + (c + 1) * fc])
        h = (a * jax.nn.sigmoid(a) * b).astype(BF16)
        acc = acc + _dot(h, wout_ref[c * fc:(c + 1) * fc, :])
    out = x + FFN_RES * acc
    if final_norm:
        out = _rms(out, fg_ref[...])
    o_ref[...] = out


def _ffn(x, g, w_in, w_out, fg, *, tm, final_norm):
    n, d = x.shape
    d_ff = w_out.shape[0]
    kern = functools.partial(_ffn_kernel, d_ff=d_ff, n_chunks=2, final_norm=final_norm)
    return pl.pallas_call(
        kern,
        grid=(n // tm,),
        in_specs=[
            pl.BlockSpec((tm, d), lambda i: (i, 0)),
            _const_spec((1, d)),
            _const_spec(w_in.shape),
            _const_spec(w_out.shape),
            _const_spec((1, d)),
        ],
        out_specs=pl.BlockSpec((tm, d), lambda i: (i, 0)),
        out_shape=jax.ShapeDtypeStruct((n, d), F32),
        compiler_params=pltpu.CompilerParams(
            dimension_semantics=("arbitrary",), vmem_limit_bytes=VMEM_LIMIT_BYTES),
        name="ffn",
    )(x, g.reshape(1, d), w_in, w_out, fg.reshape(1, d))


def _hgrn_kernel(x_ref, s0_ref, g_ref, win_ref, alb_ref, on_ref, wout_ref,
                 o_ref, s_ref,
                 st_ref, q_ref, k_ref, v_ref, b_ref, on_scr,
                 *, bb, tt, chunk, layer_j, dk, dv):
    t = pl.program_id(1)
    nt = pl.num_programs(1)
    n = bb * tt
    hdim = HG_HEADS * dk

    @pl.when(t == 0)
    def _():
        for s in range(bb):
            for h in range(HG_HEADS):
                st_ref[s, h] = s0_ref[s, h].T

    x = x_ref[...].reshape(n, x_ref.shape[-1])
    hn = _rms(x, g_ref[...]).astype(BF16)
    proj = _dot(hn, win_ref[...])
    fr = proj[:, hdim:2 * hdim]
    gate = proj[:, 2 * hdim + HG_HEADS * dv:]

    alb = alb_ref[...]
    e = jnp.exp(alb - jnp.max(alb, axis=0, keepdims=True))
    lb = jnp.sum(e[:layer_j + 1], axis=0, keepdims=True) / jnp.sum(e, axis=0, keepdims=True)

    f = lb + (1.0 - lb) * jax.nn.sigmoid(fr)
    logf = jnp.log(f)
    qs = proj[:, :hdim]
    q_ref[...] = qs * jax.nn.sigmoid(qs)
    k_ref[...] = 1.0 - f
    v_ref[...] = proj[:, 2 * hdim:2 * hdim + HG_HEADS * dv]

    row = lax.broadcasted_iota(jnp.int32, (n, n), 0)
    col = lax.broadcasted_iota(jnp.int32, (n, n), 1)
    tri = jnp.where((row // chunk == col // chunk) & (col <= row), 1.0, 0.0).astype(BF16)
    hi = logf.astype(BF16)
    lo = (logf - hi.astype(F32)).astype(BF16)
    b_ref[...] = _dot(tri, hi) + _dot(tri, lo)

    half = chunk // 2
    cps = tt // chunk
    crow = lax.broadcasted_iota(jnp.int32, (chunk, chunk), 0)
    ccol = lax.broadcasted_iota(jnp.int32, (chunk, chunk), 1)
    causal = ccol <= crow

    def chunk_body(i, carry):
        s = i // cps if cps > 1 else i
        rows = pl.ds(pl.multiple_of(i * chunk, chunk), chunk)
        qc = q_ref[rows, :]
        kc = k_ref[rows, :]
        vc = v_ref[rows, :].astype(BF16)
        bc = b_ref[rows, :]
        mid = bc[half - 1:half, :]
        last = bc[chunk - 1:chunk, :]
        q_mid = (qc * jnp.exp(bc - mid)).astype(BF16)
        k_mid = (kc * jnp.exp(mid - bc)).astype(BF16)
        q_dec = (qc * jnp.exp(bc)).astype(BF16)
        k_end = (kc * jnp.exp(last - bc)).astype(BF16)
        dec = jnp.exp(last)
        for h in range(HG_HEADS):
            kl = slice(h * dk, (h + 1) * dk)
            vl = slice(h * dv, (h + 1) * dv)
            st = st_ref[s, h]
            sc = jnp.where(causal, _dot_nt(q_mid[:, kl], k_mid[:, kl]), 0.0)
            o = _dot(sc.astype(BF16), vc[:, vl]) + _dot_nt(q_dec[:, kl], st.astype(BF16))
            st_ref[s, h] = st * dec[:, kl] + _dot_tn(vc[:, vl], k_end[:, kl])
            o = o * lax.rsqrt(jnp.mean(o * o, axis=-1, keepdims=True) + EPS)
            on_scr[rows, vl] = o
        return carry

    lax.fori_loop(0, n // chunk, chunk_body, 0)

    y = on_scr[...] * on_ref[...] * (gate * jax.nn.sigmoid(gate))
    o_ref[...] = (x + _dot(y.astype(BF16), wout_ref[...])).reshape(o_ref.shape)

    @pl.when(t == nt - 1)
    def _():
        for s in range(bb):
            for h in range(HG_HEADS):
                s_ref[s, h] = st_ref[s, h].T


def _hgrn(x, s0, s0_index, g, w_in, a_lb, onorm, w_out, *, bb, tt, chunk, layer_j):
    bsz, tlen, d = x.shape
    _, heads, dk, dv = s0.shape
    n = bb * tt
    kern = functools.partial(_hgrn_kernel, bb=bb, tt=tt, chunk=chunk, layer_j=layer_j, dk=dk, dv=dv)
    return pl.pallas_call(
        kern,
        grid=(bsz // bb, tlen // tt),
        in_specs=[
            pl.BlockSpec((bb, tt, d), lambda b, t: (b, t, 0)),
            pl.BlockSpec((bb, heads, dk, dv), lambda b, t: (s0_index(b), 0, 0, 0)),
            _const_spec((1, d)),
            _const_spec(w_in.shape),
            _const_spec(a_lb.shape),
            _const_spec((1, heads * dv)),
            _const_spec(w_out.shape),
        ],
        out_specs=[
            pl.BlockSpec((bb, tt, d), lambda b, t: (b, t, 0)),
            pl.BlockSpec((bb, heads, dk, dv), lambda b, t: (b, 0, 0, 0)),
        ],
        out_shape=[
            jax.ShapeDtypeStruct((bsz, tlen, d), F32),
            jax.ShapeDtypeStruct((bsz, heads, dk, dv), F32),
        ],
        scratch_shapes=[
            pltpu.VMEM((bb, heads, dv, dk), F32),
            pltpu.VMEM((n, heads * dk), F32),
            pltpu.VMEM((n, heads * dk), F32),
            pltpu.VMEM((n, heads * dv), F32),
            pltpu.VMEM((n, heads * dk), F32),
            pltpu.VMEM((n, heads * dv), F32),
        ],
        compiler_params=pltpu.CompilerParams(
            dimension_semantics=("arbitrary", "arbitrary"), vmem_limit_bytes=VMEM_LIMIT_BYTES),
        name="hgrn2",
    )(x, s0, g.reshape(1, d), w_in, a_lb, onorm.reshape(1, heads * dv), w_out)


def _rglru_kernel(x_ref, h0_ref, c0_ref, g_ref, win_ref, cw_ref, cb_ref, wax_ref,
                  ba_ref, bx_ref, lam_ref, wout_ref,
                  o_ref, h_ref, c_ref,
                  xb_scr, ext_scr, hs_scr,
                  *, bb, tt, d_rnn):
    t = pl.program_id(1)
    n = bb * tt
    bw = d_rnn // RG_BLOCKS
    tail = CONV_W - 1

    @pl.when(t == 0)
    def _():
        h_ref[...] = h0_ref[...]
        c_ref[...] = c0_ref[...]

    x = x_ref[...].reshape(n, x_ref.shape[-1])
    hn = _rms(x, g_ref[...]).astype(BF16)
    proj = _dot(hn, win_ref[...])
    xb_scr[...] = proj[:, :d_rnn]
    gb = proj[:, d_rnn:]

    cw = cw_ref[...]
    softplus_neg_lam = jax.nn.softplus(-lam_ref[...])
    row = lax.broadcasted_iota(jnp.int32, (tt, d_rnn), 0)

    def stream_body(s, carry):
        rows = pl.ds(pl.multiple_of(s * tt, SUBLANES), tt)
        xs = xb_scr[rows, :]
        ext_scr[SUBLANES - tail:SUBLANES, :] = c_ref[s]
        ext_scr[SUBLANES:SUBLANES + tt, :] = xs
        conv = cb_ref[...] + ext_scr[SUBLANES - tail:SUBLANES - tail + tt, :] * cw[0:1, :]
        for j in range(1, CONV_W):
            conv = conv + ext_scr[SUBLANES - tail + j:SUBLANES - tail + j + tt, :] * cw[j:j + 1, :]
        c_ref[s] = ext_scr[SUBLANES + tt - tail:SUBLANES + tt, :]

        cfb = conv.astype(BF16)
        pre = [_dot(cfb[:, i * bw:(i + 1) * bw], wax_ref[i]) for i in range(RG_BLOCKS)]
        r = jax.nn.sigmoid(jnp.concatenate([p[:, :bw] for p in pre], axis=-1) + ba_ref[...])
        ig = jax.nn.sigmoid(jnp.concatenate([p[:, bw:] for p in pre], axis=-1) + bx_ref[...])
        log_a = -RG_C * r * softplus_neg_lam
        a = jnp.exp(log_a)
        u = jnp.sqrt(-jnp.tanh(log_a) * (a * a + 1.0)) * (ig * conv)

        d = 1
        while d < tt:
            keep = row >= d
            a_sh = pltpu.roll(a, d, 0)
            u_sh = pltpu.roll(u, d, 0)
            u = jnp.where(keep, a * u_sh + u, u)
            a = jnp.where(keep, a * a_sh, a)
            d *= 2
        hs = a * h_ref[s] + u
        h_ref[s] = hs[tt - 1:tt, :]
        hs_scr[rows, :] = hs
        return carry

    lax.fori_loop(0, bb, stream_body, 0)

    y = (hs_scr[...] * jax.nn.gelu(gb)).astype(BF16)
    o_ref[...] = (x + _dot(y, wout_ref[...])).reshape(o_ref.shape)


def _rglru(x, h0, c0, st_index, g, w_in, conv_w, conv_b, wax, ba, bx, lam, w_out, *, bb, tt):
    bsz, tlen, d = x.shape
    d_rnn = h0.shape[-1]
    tail = c0.shape[1]
    n = bb * tt
    kern = functools.partial(_rglru_kernel, bb=bb, tt=tt, d_rnn=d_rnn)
    row = lambda v: v.reshape(1, d_rnn)
    return pl.pallas_call(
        kern,
        grid=(bsz // bb, tlen // tt),
        in_specs=[
            pl.BlockSpec((bb, tt, d), lambda b, t: (b, t, 0)),
            pl.BlockSpec((bb, 1, d_rnn), lambda b, t: (st_index(b), 0, 0)),
            pl.BlockSpec((bb, tail, d_rnn), lambda b, t: (st_index(b), 0, 0)),
            _const_spec((1, d)),
            _const_spec(w_in.shape),
            _const_spec(conv_w.shape),
            _const_spec((1, d_rnn)),
            _const_spec(wax.shape),
            _const_spec((1, d_rnn)),
            _const_spec((1, d_rnn)),
            _const_spec((1, d_rnn)),
            _const_spec(w_out.shape),
        ],
        out_specs=[
            pl.BlockSpec((bb, tt, d), lambda b, t: (b, t, 0)),
            pl.BlockSpec((bb, 1, d_rnn), lambda b, t: (b, 0, 0)),
            pl.BlockSpec((bb, tail, d_rnn), lambda b, t: (b, 0, 0)),
        ],
        out_shape=[
            jax.ShapeDtypeStruct((bsz, tlen, d), F32),
            jax.ShapeDtypeStruct((bsz, 1, d_rnn), F32),
            jax.ShapeDtypeStruct((bsz, tail, d_rnn), F32),
        ],
        scratch_shapes=[
            pltpu.VMEM((n, d_rnn), F32),
            pltpu.VMEM((SUBLANES + tt, d_rnn), F32),
            pltpu.VMEM((n, d_rnn), F32),
        ],
        compiler_params=pltpu.CompilerParams(
            dimension_semantics=("arbitrary", "arbitrary"), vmem_limit_bytes=VMEM_LIMIT_BYTES),
        name="rglru",
    )(x, h0, c0, g.reshape(1, d), w_in, conv_w, row(conv_b), wax, row(ba), row(bx), row(lam), w_out)


def _trunk(x, s_hgrn, s_h, s_conv, st_index, w, *, bb, tt, chunk, tm):
    bsz, tlen, d = x.shape
    depth = w["ffn_norm"].shape[0]
    new_s, new_h, new_c = [], [], []
    h = x.reshape(bsz * tlen, d)
    for layer in range(depth):
        j = layer // N_MIXERS
        h = _ffn(h, w["ffn_norm"][layer, 0], w["ffn_w_in"][layer, 0], w["ffn_w_out"][layer, 0],
                 w["final_norm"], tm=tm, final_norm=False)
        h3 = h.reshape(bsz, tlen, d)
        if layer % N_MIXERS == 0:
            h3, s_new = _hgrn(h3, s_hgrn[j], st_index, w["mix_norm"][layer], w["a_w_in"][j], w["a_lb"],
                              w["a_onorm"][j], w["a_w_out"][j], bb=bb, tt=tt, chunk=chunk, layer_j=j)
            new_s.append(s_new)
        else:
            h3, h_new, c_new = _rglru(h3, s_h[j], s_conv[j], st_index, w["mix_norm"][layer], w["b_w_in"][j],
                                      w["b_conv_w"][j], w["b_conv_b"][j], w["b_wax"][j], w["b_ba"][j],
                                      w["b_bx"][j], w["b_lambda"][j], w["b_w_out"][j], bb=bb, tt=tt)
            new_h.append(h_new)
            new_c.append(c_new)
        h = h3.reshape(bsz * tlen, d)
        h = _ffn(h, w["ffn_norm"][layer, 1], w["ffn_w_in"][layer, 1], w["ffn_w_out"][layer, 1],
                 w["final_norm"], tm=tm, final_norm=(layer == depth - 1))
    return h.reshape(bsz, tlen, d), new_s, new_h, new_c


def kernel(x_prompt, x_sample, state_hgrn, state_rglru, state_conv, meta_tokens, ffn_norm, ffn_w_in, ffn_w_out, mix_norm, a_w_in, a_lb, a_onorm, a_w_out, b_w_in, b_conv_w, b_conv_b, b_wa, b_ba, b_wx, b_bx, b_lambda, b_w_out, final_norm):
    n_dec, dec_seq, d = x_sample.shape
    n_new = x_prompt.shape[0]
    n_meta = meta_tokens.shape[0]
    assert n_meta == dec_seq, "the meta prefix is run as one more stream of the short pass"

    w = dict(
        ffn_norm=ffn_norm, mix_norm=mix_norm, final_norm=final_norm,
        ffn_w_in=ffn_w_in.astype(BF16), ffn_w_out=ffn_w_out.astype(BF16),
        a_w_in=a_w_in.astype(BF16), a_lb=a_lb, a_onorm=a_onorm, a_w_out=a_w_out.astype(BF16),
        b_w_in=b_w_in.astype(BF16), b_conv_w=b_conv_w, b_conv_b=b_conv_b,
        b_wax=jnp.concatenate([b_wa, b_wx], axis=-1).astype(BF16),
        b_ba=b_ba, b_bx=b_bx, b_lambda=b_lambda, b_w_out=b_w_out.astype(BF16),
    )

    def with_zero_stream(s):
        return jnp.concatenate([s, jnp.zeros_like(s[:, :1])], axis=1)

    xs = jnp.concatenate([x_sample, meta_tokens.astype(x_sample.dtype)[None]], axis=0)
    n_short = n_dec + 1
    ys, s_s, h_s, c_s = _trunk(
        xs, with_zero_stream(state_hgrn), with_zero_stream(state_rglru)[:, :, None, :],
        with_zero_stream(state_conv), lambda b: b, w,
        bb=n_short, tt=dec_seq, chunk=dec_seq, tm=n_short * dec_seq)

    yp, s_p, h_p, c_p = _trunk(
        x_prompt, s_s, h_s, c_s, lambda b: n_dec, w,
        bb=1, tt=256, chunk=32, tm=512)

    return (yp, ys[:n_dec],
            jnp.stack(s_p), jnp.stack([s[:n_dec] for s in s_s]),
            jnp.stack([h[:, 0] for h in h_p]), jnp.stack([h[:n_dec, 0] for h in h_s]),
            jnp.stack(c_p), jnp.stack([c[:n_dec] for c in c_s]))
```

```python
import functools

import jax
import jax.numpy as jnp
from jax import lax
from jax.experimental import pallas as pl
from jax.experimental.pallas import tpu as pltpu

F32 = jnp.float32
BF16 = jnp.bfloat16

EPS = 1e-6
FFN_RES = 0.5
RG_C = 8.0
N_MIXERS = 2
HG_HEADS = 8
RG_BLOCKS = 8
CONV_W = 4

V7X_VMEM_BYTES = 64 * 1024 * 1024
VMEM_LIMIT_BYTES = V7X_VMEM_BYTES - 8 * 1024 * 1024
SUBLANES = 8
LANES = 128
V7X_MXU_DIM = 256


def _rms(x, g):
    return x * lax.rsqrt(jnp.mean(x * x, axis=-1, keepdims=True) + EPS) * g


def _dot(a, b):
    return jnp.dot(a, b, preferred_element_type=F32)


def _dot_nt(a, b):
    return lax.dot_general(a, b, (((1,), (1,)), ((), ())), preferred_element_type=F32)


def _dot_tn(a, b):
    return lax.dot_general(a, b, (((0,), (0,)), ((), ())), preferred_element_type=F32)


def _const_spec(shape):
    zeros = (0,) * len(shape)
    return pl.BlockSpec(shape, lambda *_: zeros)


def _for_each(count, body):
    if count == 1:
        body(0)
    else:
        lax.fori_loop(0, count, lambda i, c: (body(i), c)[1], 0)


def _ffn_kernel(x_ref, g_ref, win_ref, wout_ref, fg_ref, o_ref, *, d_ff, fc, final_norm):
    x = x_ref[...]
    yb = _rms(x, g_ref[...]).astype(BF16)
    acc = None
    for lo in range(0, d_ff, fc):
        a = _dot(yb, win_ref[:, lo:lo + fc])
        b = _dot(yb, win_ref[:, d_ff + lo:d_ff + lo + fc])
        h = (a * jax.nn.sigmoid(a) * b).astype(BF16)
        p = _dot(h, wout_ref[lo:lo + fc, :])
        acc = p if acc is None else acc + p
    out = x + FFN_RES * acc
    if final_norm:
        out = _rms(out, fg_ref[...])
    o_ref[...] = out


def _ffn(x, g, w_in, w_out, fg, *, tm, final_norm):
    n, d = x.shape
    d_ff = w_out.shape[0]
    fc = V7X_MXU_DIM
    assert d_ff % fc == 0
    kern = functools.partial(_ffn_kernel, d_ff=d_ff, fc=fc, final_norm=final_norm)
    return pl.pallas_call(
        kern,
        grid=(n // tm,),
        in_specs=[
            pl.BlockSpec((tm, d), lambda i: (i, 0)),
            _const_spec((1, d)),
            _const_spec(w_in.shape),
            _const_spec(w_out.shape),
            _const_spec((1, d)),
        ],
        out_specs=pl.BlockSpec((tm, d), lambda i: (i, 0)),
        out_shape=jax.ShapeDtypeStruct((n, d), F32),
        compiler_params=pltpu.CompilerParams(
            dimension_semantics=("arbitrary",), vmem_limit_bytes=VMEM_LIMIT_BYTES),
        name="ffn",
    )(x, g.reshape(1, d), w_in, w_out, fg.reshape(1, d))


def _hgrn_kernel(x_ref, s0_ref, g_ref, win_ref, alb_ref, on_ref, wout_ref,
                 o_ref, s_ref,
                 st_ref, qm_scr, km_scr, qd_scr, ke_scr, v_scr, dec_scr, on_scr,
                 *, bb, tt, chunk, layer_j, dk, dv):
    t = pl.program_id(1)
    nt = pl.num_programs(1)
    n = bb * tt
    hdim = HG_HEADS * dk
    vdim = HG_HEADS * dv
    cps = tt // chunk
    half = chunk // 2
    sup = LANES if tt % LANES == 0 else tt

    @pl.when(t == 0)
    def _():
        for s in range(bb):
            for h in range(HG_HEADS):
                st_ref[s, h] = s0_ref[s, h].T

    x = x_ref[...].reshape(n, x_ref.shape[-1])
    hn = _rms(x, g_ref[...]).astype(BF16)
    proj = _dot(hn, win_ref[...])
    fr = proj[:, hdim:2 * hdim]
    gate = proj[:, 2 * hdim + vdim:]
    v_scr[...] = proj[:, 2 * hdim:2 * hdim + vdim].astype(BF16)

    alb = alb_ref[...]
    e = jnp.exp(alb - jnp.max(alb, axis=0, keepdims=True))
    lb = jnp.sum(e[:layer_j + 1], axis=0, keepdims=True) / jnp.sum(e, axis=0, keepdims=True)

    f = lb + (1.0 - lb) * jax.nn.sigmoid(fr)
    logf = jnp.log(f)
    qs = proj[:, :hdim]
    q = qs * jax.nn.sigmoid(qs)
    k = 1.0 - f

    row = lax.broadcasted_iota(jnp.int32, (n, n), 0)
    col = lax.broadcasted_iota(jnp.int32, (n, n), 1)
    tri = jnp.where((row // chunk == col // chunk) & (col <= row), 1.0, 0.0).astype(BF16)
    hi = logf.astype(BF16)
    lo = (logf - hi.astype(F32)).astype(BF16)
    b = _dot(tri, hi) + _dot(tri, lo)

    for c in range(n // chunk):
        sl = slice(c * chunk, (c + 1) * chunk)
        bc = b[sl]
        mid = bc[half - 1:half, :]
        last = bc[chunk - 1:chunk, :]
        qm_scr[sl, :] = (q[sl] * jnp.exp(bc - mid)).astype(BF16)
        km_scr[sl, :] = (k[sl] * jnp.exp(mid - bc)).astype(BF16)
        qd_scr[sl, :] = (q[sl] * jnp.exp(bc)).astype(BF16)
        ke_scr[sl, :] = (k[sl] * jnp.exp(last - bc)).astype(BF16)
        dec_scr[c] = jnp.exp(last)

    srow = lax.broadcasted_iota(jnp.int32, (sup, sup), 0)
    scol = lax.broadcasted_iota(jnp.int32, (sup, sup), 1)
    diag_causal = (srow // chunk == scol // chunk) & (scol <= srow)

    def stream_body(s):
        base = s * tt
        for h in range(HG_HEADS):
            kl = slice(h * dk, (h + 1) * dk)
            vl = slice(h * dv, (h + 1) * dv)
            intra = []
            for p in range(tt // sup):
                rows = pl.ds(pl.multiple_of(base + p * sup, chunk), sup)
                sc = jnp.where(diag_causal, _dot_nt(qm_scr[rows, kl], km_scr[rows, kl]), 0.0)
                intra.append(_dot(sc.astype(BF16), v_scr[rows, vl]))
            st = st_ref[s, h]
            for c in range(cps):
                rows = pl.ds(pl.multiple_of(base + c * chunk, chunk), chunk)
                p, off = divmod(c * chunk, sup)
                o = intra[p][off:off + chunk] + _dot_nt(qd_scr[rows, kl], st.astype(BF16))
                dec = dec_scr[s * cps + c, :, kl]
                st = st * dec + _dot_tn(v_scr[rows, vl], ke_scr[rows, kl])
                o = o * lax.rsqrt(jnp.mean(o * o, axis=-1, keepdims=True) + EPS)
                on_scr[rows, vl] = o
            st_ref[s, h] = st

    _for_each(bb, stream_body)

    y = on_scr[...] * on_ref[...] * (gate * jax.nn.sigmoid(gate))
    o_ref[...] = (x + _dot(y.astype(BF16), wout_ref[...])).reshape(o_ref.shape)

    @pl.when(t == nt - 1)
    def _():
        for s in range(bb):
            for h in range(HG_HEADS):
                s_ref[s, h] = st_ref[s, h].T


def _hgrn(x, s0, s0_index, g, w_in, a_lb, onorm, w_out, *, bb, tt, chunk, layer_j):
    bsz, tlen, d = x.shape
    _, heads, dk, dv = s0.shape
    n = bb * tt
    kern = functools.partial(_hgrn_kernel, bb=bb, tt=tt, chunk=chunk, layer_j=layer_j, dk=dk, dv=dv)
    return pl.pallas_call(
        kern,
        grid=(bsz // bb, tlen // tt),
        in_specs=[
            pl.BlockSpec((bb, tt, d), lambda b, t: (b, t, 0)),
            pl.BlockSpec((bb, heads, dk, dv), lambda b, t: (s0_index(b), 0, 0, 0)),
            _const_spec((1, d)),
            _const_spec(w_in.shape),
            _const_spec(a_lb.shape),
            _const_spec((1, heads * dv)),
            _const_spec(w_out.shape),
        ],
        out_specs=[
            pl.BlockSpec((bb, tt, d), lambda b, t: (b, t, 0)),
            pl.BlockSpec((bb, heads, dk, dv), lambda b, t: (b, 0, 0, 0)),
        ],
        out_shape=[
            jax.ShapeDtypeStruct((bsz, tlen, d), F32),
            jax.ShapeDtypeStruct((bsz, heads, dk, dv), F32),
        ],
        scratch_shapes=[
            pltpu.VMEM((bb, heads, dv, dk), F32),
            pltpu.VMEM((n, heads * dk), BF16),
            pltpu.VMEM((n, heads * dk), BF16),
            pltpu.VMEM((n, heads * dk), BF16),
            pltpu.VMEM((n, heads * dk), BF16),
            pltpu.VMEM((n, heads * dv), BF16),
            pltpu.VMEM((n // chunk, 1, heads * dk), F32),
            pltpu.VMEM((n, heads * dv), F32),
        ],
        compiler_params=pltpu.CompilerParams(
            dimension_semantics=("arbitrary", "arbitrary"), vmem_limit_bytes=VMEM_LIMIT_BYTES),
        name="hgrn2",
    )(x, s0, g.reshape(1, d), w_in, a_lb, onorm.reshape(1, heads * dv), w_out)


def _segment_perm(n, tt, transpose):
    seg = tt // SUBLANES
    r = lax.broadcasted_iota(jnp.int32, (n, n), 1 if transpose else 0)
    c = lax.broadcasted_iota(jnp.int32, (n, n), 0 if transpose else 1)
    loc = r % tt
    src = (r - loc) + (loc % SUBLANES) * seg + loc // SUBLANES
    return jnp.where(c == src, 1.0, 0.0).astype(BF16)


def _rglru_kernel(x_ref, h0_ref, c0_ref, g_ref, win_ref, cw_ref, cb_ref, wax_ref,
                  ba_ref, bx_ref, lam_ref, wout_ref,
                  o_ref, h_ref, c_ref,
                  xb_scr, hs_scr,
                  *, bb, tt, d_rnn):
    t = pl.program_id(1)
    n = bb * tt
    bw = d_rnn // RG_BLOCKS
    tail = CONV_W - 1
    seg = tt // SUBLANES

    @pl.when(t == 0)
    def _():
        h_ref[...] = h0_ref[...]
        c_ref[...] = c0_ref[...]

    x = x_ref[...].reshape(n, x_ref.shape[-1])
    hn = _rms(x, g_ref[...]).astype(BF16)
    hp = _dot(_segment_perm(n, tt, False), hn).astype(BF16)
    proj = _dot(hp, win_ref[...])
    xb_scr[...] = proj[:, :d_rnn]
    gb = proj[:, d_rnn:]

    cw = cw_ref[...]
    softplus_neg_lam = jax.nn.softplus(-lam_ref[...])
    sub = lax.broadcasted_iota(jnp.int32, (SUBLANES, d_rnn), 0)

    def stream_body(s):
        base = s * tt
        xg = [xb_scr[pl.ds(pl.multiple_of(base + g * SUBLANES, SUBLANES), SUBLANES), :] for g in range(seg)]
        prev = c_ref[s]
        h0 = h_ref[s]

        def delayed(g, j):
            gg, wraps = g - j, 0
            while gg < 0:
                gg, wraps = gg + seg, wraps + 1
            v = xg[gg]
            if wraps:
                v = pltpu.roll(v, wraps, 0)
                for sl in range(wraps):
                    i = tail + sl * seg + g - j
                    v = jnp.where(sub == sl, prev[i:i + 1, :], v)
            return v

        conv = []
        for g in range(seg):
            acc = cb_ref[...] + delayed(g, tail) * cw[0:1, :]
            for j in range(1, CONV_W):
                acc = acc + delayed(g, tail - j) * cw[j:j + 1, :]
            conv.append(acc)
        for i in range(tail):
            step = tt - tail + i
            c_ref[s, i:i + 1, :] = xg[step % seg][step // seg:step // seg + 1, :]
        conv = jnp.concatenate(conv, axis=0)

        cfb = conv.astype(BF16)
        pre = [_dot(cfb[:, i * bw:(i + 1) * bw], wax_ref[i]) for i in range(RG_BLOCKS)]
        r = jax.nn.sigmoid(jnp.concatenate([p[:, :bw] for p in pre], axis=-1) + ba_ref[...])
        ig = jax.nn.sigmoid(jnp.concatenate([p[:, bw:] for p in pre], axis=-1) + bx_ref[...])
        log_a = -RG_C * r * softplus_neg_lam
        a = jnp.exp(log_a)
        u = jnp.sqrt(-jnp.tanh(log_a) * (a * a + 1.0)) * (ig * conv)

        hz = [u[0:SUBLANES]]
        az = [a[0:SUBLANES]]
        for g in range(1, seg):
            ag = a[g * SUBLANES:(g + 1) * SUBLANES]
            hz.append(ag * hz[-1] + u[g * SUBLANES:(g + 1) * SUBLANES])
            az.append(ag * az[-1])
        e_end, a_end = hz[-1], az[-1]
        d = 1
        while d < SUBLANES:
            keep = sub >= d
            e_sh = pltpu.roll(e_end, d, 0)
            a_sh = pltpu.roll(a_end, d, 0)
            e_end = jnp.where(keep, a_end * e_sh + e_end, e_end)
            a_end = jnp.where(keep, a_end * a_sh, a_end)
            d *= 2
        h_end = a_end * h0 + e_end
        h_start = jnp.where(sub == 0, h0, pltpu.roll(h_end, 1, 0))
        h_ref[s] = h_end[SUBLANES - 1:SUBLANES, :]
        for g in range(seg):
            rows = pl.ds(pl.multiple_of(base + g * SUBLANES, SUBLANES), SUBLANES)
            hs_scr[rows, :] = hz[g] + az[g] * h_start

    _for_each(bb, stream_body)

    y = (hs_scr[...] * jax.nn.gelu(gb)).astype(BF16)
    yn = _dot(_segment_perm(n, tt, True), y).astype(BF16)
    o_ref[...] = (x + _dot(yn, wout_ref[...])).reshape(o_ref.shape)


def _rglru(x, h0, c0, st_index, g, w_in, conv_w, conv_b, wax, ba, bx, lam, w_out, *, bb, tt):
    bsz, tlen, d = x.shape
    d_rnn = h0.shape[-1]
    tail = c0.shape[1]
    n = bb * tt
    kern = functools.partial(_rglru_kernel, bb=bb, tt=tt, d_rnn=d_rnn)
    row = lambda v: v.reshape(1, d_rnn)
    return pl.pallas_call(
        kern,
        grid=(bsz // bb, tlen // tt),
        in_specs=[
            pl.BlockSpec((bb, tt, d), lambda b, t: (b, t, 0)),
            pl.BlockSpec((bb, 1, d_rnn), lambda b, t: (st_index(b), 0, 0)),
            pl.BlockSpec((bb, tail, d_rnn), lambda b, t: (st_index(b), 0, 0)),
            _const_spec((1, d)),
            _const_spec(w_in.shape),
            _const_spec(conv_w.shape),
            _const_spec((1, d_rnn)),
            _const_spec(wax.shape),
            _const_spec((1, d_rnn)),
            _const_spec((1, d_rnn)),
            _const_spec((1, d_rnn)),
            _const_spec(w_out.shape),
        ],
        out_specs=[
            pl.BlockSpec((bb, tt, d), lambda b, t: (b, t, 0)),
            pl.BlockSpec((bb, 1, d_rnn), lambda b, t: (b, 0, 0)),
            pl.BlockSpec((bb, tail, d_rnn), lambda b, t: (b, 0, 0)),
        ],
        out_shape=[
            jax.ShapeDtypeStruct((bsz, tlen, d), F32),
            jax.ShapeDtypeStruct((bsz, 1, d_rnn), F32),
            jax.ShapeDtypeStruct((bsz, tail, d_rnn), F32),
        ],
        scratch_shapes=[
            pltpu.VMEM((n, d_rnn), F32),
            pltpu.VMEM((n, d_rnn), F32),
        ],
        compiler_params=pltpu.CompilerParams(
            dimension_semantics=("arbitrary", "arbitrary"), vmem_limit_bytes=VMEM_LIMIT_BYTES),
        name="rglru",
    )(x, h0, c0, g.reshape(1, d), w_in, conv_w, row(conv_b), wax, row(ba), row(bx), row(lam), w_out)


def _trunk(x, s_hgrn, s_h, s_conv, st_index, w, *, bb, tt, chunk, tm):
    bsz, tlen, d = x.shape
    depth = w["ffn_norm"].shape[0]
    new_s, new_h, new_c = [], [], []
    h = x.reshape(bsz * tlen, d)
    for layer in range(depth):
        j = layer // N_MIXERS
        h = _ffn(h, w["ffn_norm"][layer, 0], w["ffn_w_in"][layer, 0], w["ffn_w_out"][layer, 0],
                 w["final_norm"], tm=tm, final_norm=False)
        h3 = h.reshape(bsz, tlen, d)
        if layer % N_MIXERS == 0:
            h3, s_new = _hgrn(h3, s_hgrn[j], st_index, w["mix_norm"][layer], w["a_w_in"][j], w["a_lb"],
                              w["a_onorm"][j], w["a_w_out"][j], bb=bb, tt=tt, chunk=chunk, layer_j=j)
            new_s.append(s_new)
        else:
            h3, h_new, c_new = _rglru(h3, s_h[j], s_conv[j], st_index, w["mix_norm"][layer], w["b_w_in"][j],
                                      w["b_conv_w"][j], w["b_conv_b"][j], w["b_wax"][j], w["b_ba"][j],
                                      w["b_bx"][j], w["b_lambda"][j], w["b_w_out"][j], bb=bb, tt=tt)
            new_h.append(h_new)
            new_c.append(c_new)
        h = h3.reshape(bsz * tlen, d)
        h = _ffn(h, w["ffn_norm"][layer, 1], w["ffn_w_in"][layer, 1], w["ffn_w_out"][layer, 1],
                 w["final_norm"], tm=tm, final_norm=(layer == depth - 1))
    return h.reshape(bsz, tlen, d), new_s, new_h, new_c


def kernel(x_prompt, x_sample, state_hgrn, state_rglru, state_conv, meta_tokens, ffn_norm, ffn_w_in, ffn_w_out, mix_norm, a_w_in, a_lb, a_onorm, a_w_out, b_w_in, b_conv_w, b_conv_b, b_wa, b_ba, b_wx, b_bx, b_lambda, b_w_out, final_norm):
    n_dec, dec_seq, d = x_sample.shape
    n_new = x_prompt.shape[0]
    n_meta = meta_tokens.shape[0]
    assert n_meta == dec_seq, "the meta prefix is run as one more stream of the short pass"

    w = dict(
        ffn_norm=ffn_norm, mix_norm=mix_norm, final_norm=final_norm,
        ffn_w_in=ffn_w_in.astype(BF16), ffn_w_out=ffn_w_out.astype(BF16),
        a_w_in=a_w_in.astype(BF16), a_lb=a_lb, a_onorm=a_onorm, a_w_out=a_w_out.astype(BF16),
        b_w_in=b_w_in.astype(BF16), b_conv_w=b_conv_w, b_conv_b=b_conv_b,
        b_wax=jnp.concatenate([b_wa, b_wx], axis=-1).astype(BF16),
        b_ba=b_ba, b_bx=b_bx, b_lambda=b_lambda, b_w_out=b_w_out.astype(BF16),
    )

    def with_zero_stream(s):
        return jnp.concatenate([s, jnp.zeros_like(s[:, :1])], axis=1)

    xs = jnp.concatenate([x_sample, meta_tokens.astype(x_sample.dtype)[None]], axis=0)
    n_short = n_dec + 1
    ys, s_s, h_s, c_s = _trunk(
        xs, with_zero_stream(state_hgrn), with_zero_stream(state_rglru)[:, :, None, :],
        with_zero_stream(state_conv), lambda b: b, w,
        bb=n_short, tt=dec_seq, chunk=dec_seq, tm=n_short * dec_seq)

    yp, s_p, h_p, c_p = _trunk(
        x_prompt, s_s, h_s, c_s, lambda b: n_dec, w,
        bb=1, tt=256, chunk=32, tm=512)

    return (yp, ys[:n_dec],
            jnp.stack(s_p), jnp.stack([s[:n_dec] for s in s_s]),
            jnp.stack([h[:, 0] for h in h_p]), jnp.stack([h[:n_dec, 0] for h in h_s]),
            jnp.stack(c_p), jnp.stack([c[:n_dec] for c in c_s]))
```

```python
import functools
import itertools

import jax
import jax.numpy as jnp
from jax import lax
from jax.experimental import pallas as pl
from jax.experimental.pallas import tpu as pltpu

F32 = jnp.float32
BF16 = jnp.bfloat16

EPS = 1e-6
FFN_RES = 0.5
RG_C = 8.0
N_MIXERS = 2
HG_HEADS = 8
RG_BLOCKS = 8
CONV_W = 4

V7X_VMEM_BYTES = 64 * 1024 * 1024
VMEM_LIMIT_BYTES = V7X_VMEM_BYTES - 8 * 1024 * 1024
SUBLANES = 8
LANES = 128
V7X_MXU_DIM = 256


def _rms(x, g):
    return x * lax.rsqrt(jnp.mean(x * x, axis=-1, keepdims=True) + EPS) * g


def _dot(a, b):
    return jnp.dot(a, b, preferred_element_type=F32)


def _dot_nt(a, b):
    return lax.dot_general(a, b, (((1,), (1,)), ((), ())), preferred_element_type=F32)


def _dot_tn(a, b):
    return lax.dot_general(a, b, (((0,), (0,)), ((), ())), preferred_element_type=F32)


def _const_spec(shape):
    zeros = (0,) * len(shape)
    return pl.BlockSpec(shape, lambda *_: zeros)


def _pick_spec(shape, lead):
    tail = shape[len(lead):]
    index = tuple(lead) + (0,) * len(tail)
    return pl.BlockSpec((None,) * len(lead) + tuple(tail), lambda *_: index)


def _for_each(count, body):
    if count == 1:
        body(0)
    else:
        lax.fori_loop(0, count, lambda i, c: (body(i), c)[1], 0)


def _run_interleaved(chains):
    for _ in itertools.zip_longest(*chains):
        pass


def _ffn_kernel(x_ref, g_ref, win_ref, wout_ref, fg_ref, o_ref, *, d_ff, fc, final_norm):
    x = x_ref[...]
    yb = _rms(x, g_ref[...]).astype(BF16)
    acc = None
    for lo in range(0, d_ff, fc):
        a = _dot(yb, win_ref[:, lo:lo + fc])
        b = _dot(yb, win_ref[:, d_ff + lo:d_ff + lo + fc])
        h = (a * jax.nn.sigmoid(a) * b).astype(BF16)
        p = _dot(h, wout_ref[lo:lo + fc, :])
        acc = p if acc is None else acc + p
    out = x + FFN_RES * acc
    if final_norm:
        out = _rms(out, fg_ref[...])
    o_ref[...] = out


def _ffn(x, g, w_in, w_out, lead, fg, *, tm, final_norm):
    n, d = x.shape
    d_ff = w_out.shape[-2]
    fc = V7X_MXU_DIM
    assert d_ff % fc == 0
    kern = functools.partial(_ffn_kernel, d_ff=d_ff, fc=fc, final_norm=final_norm)
    return pl.pallas_call(
        kern,
        grid=(n // tm,),
        in_specs=[
            pl.BlockSpec((tm, d), lambda i: (i, 0)),
            _const_spec((1, d)),
            _pick_spec(w_in.shape, lead),
            _pick_spec(w_out.shape, lead),
            _const_spec((1, d)),
        ],
        out_specs=pl.BlockSpec((tm, d), lambda i: (i, 0)),
        out_shape=jax.ShapeDtypeStruct((n, d), F32),
        compiler_params=pltpu.CompilerParams(
            dimension_semantics=("arbitrary",), vmem_limit_bytes=VMEM_LIMIT_BYTES),
        name="ffn",
    )(x, g.reshape(1, d), w_in, w_out, fg.reshape(1, d))


def _hgrn_kernel(x_ref, s0_ref, g_ref, win_ref, alb_ref, on_ref, wout_ref,
                 o_ref, s_ref,
                 st_ref, qm_scr, km_scr, qd_scr, ke_scr, v_scr, dec_scr, on_scr,
                 *, bb, gs, tt, chunk, layer_j, dk, dv, shared_init):
    t = pl.program_id(1)
    nt = pl.num_programs(1)
    gn = gs * tt
    hdim = HG_HEADS * dk
    vdim = HG_HEADS * dv
    cps = tt // chunk
    half = chunk // 2
    sup = LANES if tt % LANES == 0 else tt

    @pl.when(t == 0)
    def _():
        for s in range(bb):
            for h in range(HG_HEADS):
                st_ref[s, h] = s0_ref[0 if shared_init else s, h].T

    alb = alb_ref[...]
    e = jnp.exp(alb - jnp.max(alb, axis=0, keepdims=True))
    lb = jnp.sum(e[:layer_j + 1], axis=0, keepdims=True) / jnp.sum(e, axis=0, keepdims=True)

    row = lax.broadcasted_iota(jnp.int32, (gn, gn), 0)
    col = lax.broadcasted_iota(jnp.int32, (gn, gn), 1)
    tri = jnp.where((row // chunk == col // chunk) & (col <= row), 1.0, 0.0).astype(BF16)
    srow = lax.broadcasted_iota(jnp.int32, (sup, sup), 0)
    scol = lax.broadcasted_iota(jnp.int32, (sup, sup), 1)
    diag_causal = (srow // chunk == scol // chunk) & (scol <= srow)

    def recurrence(s):
        base = s * tt
        heads = range(HG_HEADS)
        kls = [slice(h * dk, (h + 1) * dk) for h in heads]
        vls = [slice(h * dv, (h + 1) * dv) for h in heads]
        sup_rows = [pl.ds(pl.multiple_of(base + p * sup, chunk), sup) for p in range(tt // sup)]
        chunk_rows = [pl.ds(pl.multiple_of(base + c * chunk, chunk), chunk) for c in range(cps)]
        sc = [[_dot_nt(qm_scr[r, kls[h]], km_scr[r, kls[h]]) for r in sup_rows] for h in heads]
        upd = [[_dot_tn(v_scr[r, vls[h]], ke_scr[r, kls[h]]) for r in chunk_rows] for h in heads]
        intra = [[_dot(jnp.where(diag_causal, sc[h][p], 0.0).astype(BF16), v_scr[r, vls[h]])
                  for p, r in enumerate(sup_rows)] for h in heads]
        states = []
        for h in heads:
            st = st_ref[s, h]
            at_chunk_start = []
            for c in range(cps):
                at_chunk_start.append(st.astype(BF16))
                st = st * dec_scr[s * cps + c, :, kls[h]] + upd[h][c]
            st_ref[s, h] = st
            states.append(at_chunk_start)
        for h in heads:
            for c, r in enumerate(chunk_rows):
                p, off = divmod(c * chunk, sup)
                o = intra[h][p][off:off + chunk] + _dot_nt(qd_scr[r, kls[h]], states[h][c])
                on_scr[r, vls[h]] = o * lax.rsqrt(jnp.mean(o * o, axis=-1, keepdims=True) + EPS)

    def chain(grp):
        r0 = grp * gn
        x = x_ref[grp * gs:(grp + 1) * gs].reshape(gn, x_ref.shape[-1])
        hn = _rms(x, g_ref[...]).astype(BF16)
        proj = _dot(hn, win_ref[...])
        fr = proj[:, hdim:2 * hdim]
        gate = proj[:, 2 * hdim + vdim:]
        v_scr[r0:r0 + gn, :] = proj[:, 2 * hdim:2 * hdim + vdim].astype(BF16)
        yield

        f = lb + (1.0 - lb) * jax.nn.sigmoid(fr)
        logf = jnp.log(f)
        qs = proj[:, :hdim]
        q = qs * jax.nn.sigmoid(qs)
        k = 1.0 - f
        hi = logf.astype(BF16)
        lo = (logf - hi.astype(F32)).astype(BF16)
        b = _dot(tri, hi) + _dot(tri, lo)
        for c in range(gn // chunk):
            sl = slice(c * chunk, (c + 1) * chunk)
            dst = slice(r0 + c * chunk, r0 + (c + 1) * chunk)
            bc = b[sl]
            mid = bc[half - 1:half, :]
            last = bc[chunk - 1:chunk, :]
            qm_scr[dst, :] = (q[sl] * jnp.exp(bc - mid)).astype(BF16)
            km_scr[dst, :] = (k[sl] * jnp.exp(mid - bc)).astype(BF16)
            qd_scr[dst, :] = (q[sl] * jnp.exp(bc)).astype(BF16)
            ke_scr[dst, :] = (k[sl] * jnp.exp(last - bc)).astype(BF16)
            dec_scr[r0 // chunk + c] = jnp.exp(last)
        yield

        _for_each(gs, lambda i: recurrence(grp * gs + i))
        yield

        y = on_scr[r0:r0 + gn, :] * on_ref[...] * (gate * jax.nn.sigmoid(gate))
        o_ref[grp * gs:(grp + 1) * gs] = (x + _dot(y.astype(BF16), wout_ref[...])).reshape(gs, tt, x.shape[-1])

    _run_interleaved([chain(grp) for grp in range(bb // gs)])

    @pl.when(t == nt - 1)
    def _():
        for s in range(bb):
            for h in range(HG_HEADS):
                s_ref[s, h] = st_ref[s, h].T


def _hgrn(x, s0, shared_init, g, w_in, a_lb, onorm, w_out, lead, *, bb, gs, tt, chunk, layer_j):
    bsz, tlen, d = x.shape
    _, heads, dk, dv = s0.shape
    n = bb * tt
    shared = shared_init is not None
    kern = functools.partial(_hgrn_kernel, bb=bb, gs=gs, tt=tt, chunk=chunk, layer_j=layer_j, dk=dk, dv=dv,
                             shared_init=shared)
    if shared:
        s0_spec = pl.BlockSpec((1, heads, dk, dv), lambda b, t: (shared_init, 0, 0, 0))
    else:
        s0_spec = pl.BlockSpec((bb, heads, dk, dv), lambda b, t: (b, 0, 0, 0))
    return pl.pallas_call(
        kern,
        grid=(bsz // bb, tlen // tt),
        in_specs=[
            pl.BlockSpec((bb, tt, d), lambda b, t: (b, t, 0)),
            s0_spec,
            _const_spec((1, d)),
            _pick_spec(w_in.shape, lead),
            _const_spec(a_lb.shape),
            _const_spec((1, heads * dv)),
            _pick_spec(w_out.shape, lead),
        ],
        out_specs=[
            pl.BlockSpec((bb, tt, d), lambda b, t: (b, t, 0)),
            pl.BlockSpec((bb, heads, dk, dv), lambda b, t: (b, 0, 0, 0)),
        ],
        out_shape=[
            jax.ShapeDtypeStruct((bsz, tlen, d), F32),
            jax.ShapeDtypeStruct((bsz, heads, dk, dv), F32),
        ],
        scratch_shapes=[
            pltpu.VMEM((bb, heads, dv, dk), F32),
            pltpu.VMEM((n, heads * dk), BF16),
            pltpu.VMEM((n, heads * dk), BF16),
            pltpu.VMEM((n, heads * dk), BF16),
            pltpu.VMEM((n, heads * dk), BF16),
            pltpu.VMEM((n, heads * dv), BF16),
            pltpu.VMEM((n // chunk, 1, heads * dk), F32),
            pltpu.VMEM((n, heads * dv), F32),
        ],
        compiler_params=pltpu.CompilerParams(
            dimension_semantics=("arbitrary", "arbitrary"), vmem_limit_bytes=VMEM_LIMIT_BYTES),
        name="hgrn2",
    )(x, s0, g.reshape(1, d), w_in, a_lb, onorm.reshape(1, heads * dv), w_out)


def _segment_perm(n, tt, transpose):
    seg = tt // SUBLANES
    r = lax.broadcasted_iota(jnp.int32, (n, n), 1 if transpose else 0)
    c = lax.broadcasted_iota(jnp.int32, (n, n), 0 if transpose else 1)
    loc = r % tt
    src = (r - loc) + (loc % SUBLANES) * seg + loc // SUBLANES
    return jnp.where(c == src, 1.0, 0.0).astype(BF16)


def _rglru_kernel(x_ref, h0_ref, c0_ref, g_ref, win_ref, cw_ref, cb_ref, wax_ref,
                  ba_ref, bx_ref, lam_ref, wout_ref,
                  o_ref, h_ref, c_ref,
                  xb_scr, hs_scr,
                  *, bb, gs, tt, d_rnn, shared_init):
    t = pl.program_id(1)
    gn = gs * tt
    bw = d_rnn // RG_BLOCKS
    tail = CONV_W - 1
    seg = tt // SUBLANES

    @pl.when(t == 0)
    def _():
        for s in range(bb):
            h_ref[s] = h0_ref[0 if shared_init else s]
            c_ref[s] = c0_ref[0 if shared_init else s]

    cw = cw_ref[...]
    softplus_neg_lam = jax.nn.softplus(-lam_ref[...])
    sub = lax.broadcasted_iota(jnp.int32, (SUBLANES, d_rnn), 0)
    to_segments = _segment_perm(gn, tt, False)
    from_segments = _segment_perm(gn, tt, True)

    def stream_pieces(s):
        base = s * tt
        xg = [xb_scr[pl.ds(pl.multiple_of(base + g * SUBLANES, SUBLANES), SUBLANES), :] for g in range(seg)]
        prev = c_ref[s]
        h0 = h_ref[s]

        def delayed(g, j):
            gg, wraps = g - j, 0
            while gg < 0:
                gg, wraps = gg + seg, wraps + 1
            v = xg[gg]
            if wraps:
                v = pltpu.roll(v, wraps, 0)
                for sl in range(wraps):
                    i = tail + sl * seg + g - j
                    v = jnp.where(sub == sl, prev[i:i + 1, :], v)
            return v

        conv = []
        for g in range(seg):
            acc = cb_ref[...] + delayed(g, tail) * cw[0:1, :]
            for j in range(1, CONV_W):
                acc = acc + delayed(g, tail - j) * cw[j:j + 1, :]
            conv.append(acc)
        for i in range(tail):
            step = tt - tail + i
            c_ref[s, i:i + 1, :] = xg[step % seg][step // seg:step // seg + 1, :]
        conv = jnp.concatenate(conv, axis=0)
        yield

        cfb = conv.astype(BF16)
        pre = [_dot(cfb[:, i * bw:(i + 1) * bw], wax_ref[i]) for i in range(RG_BLOCKS)]
        r = jax.nn.sigmoid(jnp.concatenate([p[:, :bw] for p in pre], axis=-1) + ba_ref[...])
        ig = jax.nn.sigmoid(jnp.concatenate([p[:, bw:] for p in pre], axis=-1) + bx_ref[...])
        yield
        log_a = -RG_C * r * softplus_neg_lam
        a = jnp.exp(log_a)
        u = jnp.sqrt(-jnp.tanh(log_a) * (a * a + 1.0)) * (ig * conv)
        yield

        hz = [u[0:SUBLANES]]
        az = [a[0:SUBLANES]]
        for g in range(1, seg):
            ag = a[g * SUBLANES:(g + 1) * SUBLANES]
            hz.append(ag * hz[-1] + u[g * SUBLANES:(g + 1) * SUBLANES])
            az.append(ag * az[-1])
        e_end, a_end = hz[-1], az[-1]
        d = 1
        while d < SUBLANES:
            keep = sub >= d
            e_sh = pltpu.roll(e_end, d, 0)
            a_sh = pltpu.roll(a_end, d, 0)
            e_end = jnp.where(keep, a_end * e_sh + e_end, e_end)
            a_end = jnp.where(keep, a_end * a_sh, a_end)
            d *= 2
        h_end = a_end * h0 + e_end
        h_start = jnp.where(sub == 0, h0, pltpu.roll(h_end, 1, 0))
        h_ref[s] = h_end[SUBLANES - 1:SUBLANES, :]
        for g in range(seg):
            rows = pl.ds(pl.multiple_of(base + g * SUBLANES, SUBLANES), SUBLANES)
            hs_scr[rows, :] = hz[g] + az[g] * h_start

    def chain(grp):
        r0 = grp * gn
        x = x_ref[grp * gs:(grp + 1) * gs].reshape(gn, x_ref.shape[-1])
        hn = _rms(x, g_ref[...]).astype(BF16)
        hp = _dot(to_segments, hn).astype(BF16)
        yield
        xb_scr[r0:r0 + gn, :] = _dot(hp, win_ref[:, :d_rnn])
        yield
        gb = _dot(hp, win_ref[:, d_rnn:])
        yield

        if gs == 1:
            yield from stream_pieces(grp)
        else:
            _for_each(gs, lambda i: list(stream_pieces(grp * gs + i)))
        yield

        y = (hs_scr[r0:r0 + gn, :] * jax.nn.gelu(gb)).astype(BF16)
        yield
        yn = _dot(from_segments, y).astype(BF16)
        o_ref[grp * gs:(grp + 1) * gs] = (x + _dot(yn, wout_ref[...])).reshape(gs, tt, x.shape[-1])

    _run_interleaved([chain(grp) for grp in range(bb // gs)])


def _rglru(x, h0, c0, shared_init, g, w_in, conv_w, conv_b, wax, ba, bx, lam, w_out, lead, *, bb, gs, tt):
    bsz, tlen, d = x.shape
    d_rnn = h0.shape[-1]
    tail = c0.shape[1]
    n = bb * tt
    shared = shared_init is not None
    kern = functools.partial(_rglru_kernel, bb=bb, gs=gs, tt=tt, d_rnn=d_rnn, shared_init=shared)
    row = lambda v: v.reshape(1, d_rnn)
    if shared:
        init_map, init_n = (lambda b, t: (shared_init, 0, 0)), 1
    else:
        init_map, init_n = (lambda b, t: (b, 0, 0)), bb
    return pl.pallas_call(
        kern,
        grid=(bsz // bb, tlen // tt),
        in_specs=[
            pl.BlockSpec((bb, tt, d), lambda b, t: (b, t, 0)),
            pl.BlockSpec((init_n, 1, d_rnn), init_map),
            pl.BlockSpec((init_n, tail, d_rnn), init_map),
            _const_spec((1, d)),
            _pick_spec(w_in.shape, lead),
            _const_spec(conv_w.shape),
            _const_spec((1, d_rnn)),
            _pick_spec(wax.shape, lead),
            _const_spec((1, d_rnn)),
            _const_spec((1, d_rnn)),
            _const_spec((1, d_rnn)),
            _pick_spec(w_out.shape, lead),
        ],
        out_specs=[
            pl.BlockSpec((bb, tt, d), lambda b, t: (b, t, 0)),
            pl.BlockSpec((bb, 1, d_rnn), lambda b, t: (b, 0, 0)),
            pl.BlockSpec((bb, tail, d_rnn), lambda b, t: (b, 0, 0)),
        ],
        out_shape=[
            jax.ShapeDtypeStruct((bsz, tlen, d), F32),
            jax.ShapeDtypeStruct((bsz, 1, d_rnn), F32),
            jax.ShapeDtypeStruct((bsz, tail, d_rnn), F32),
        ],
        scratch_shapes=[
            pltpu.VMEM((n, d_rnn), F32),
            pltpu.VMEM((n, d_rnn), F32),
        ],
        compiler_params=pltpu.CompilerParams(
            dimension_semantics=("arbitrary", "arbitrary"), vmem_limit_bytes=VMEM_LIMIT_BYTES),
        name="rglru",
    )(x, h0, c0, g.reshape(1, d), w_in, conv_w, row(conv_b), wax, row(ba), row(bx), row(lam), w_out)


def _trunk(x, s_hgrn, s_h, s_conv, shared_init, w, *, hgrn_tiles, rglru_tiles, tm):
    bsz, tlen, d = x.shape
    depth = w["ffn_norm"].shape[0]
    new_s, new_h, new_c = [], [], []
    h = x.reshape(bsz * tlen, d)
    for layer in range(depth):
        j = layer // N_MIXERS
        h = _ffn(h, w["ffn_norm"][layer, 0], w["ffn_w_in"], w["ffn_w_out"], (layer, 0),
                 w["final_norm"], tm=tm, final_norm=False)
        h3 = h.reshape(bsz, tlen, d)
        if layer % N_MIXERS == 0:
            h3, s_new = _hgrn(h3, s_hgrn[j], shared_init, w["mix_norm"][layer], w["a_w_in"], w["a_lb"],
                              w["a_onorm"][j], w["a_w_out"], (j,), layer_j=j, **hgrn_tiles)
            new_s.append(s_new)
        else:
            h3, h_new, c_new = _rglru(h3, s_h[j], s_conv[j], shared_init, w["mix_norm"][layer], w["b_w_in"],
                                      w["b_conv_w"][j], w["b_conv_b"][j], w["b_wax"], w["b_ba"][j],
                                      w["b_bx"][j], w["b_lambda"][j], w["b_w_out"], (j,), **rglru_tiles)
            new_h.append(h_new)
            new_c.append(c_new)
        h = h3.reshape(bsz * tlen, d)
        h = _ffn(h, w["ffn_norm"][layer, 1], w["ffn_w_in"], w["ffn_w_out"], (layer, 1),
                 w["final_norm"], tm=tm, final_norm=(layer == depth - 1))
    return h.reshape(bsz, tlen, d), new_s, new_h, new_c


def kernel(x_prompt, x_sample, state_hgrn, state_rglru, state_conv, meta_tokens, ffn_norm, ffn_w_in, ffn_w_out, mix_norm, a_w_in, a_lb, a_onorm, a_w_out, b_w_in, b_conv_w, b_conv_b, b_wa, b_ba, b_wx, b_bx, b_lambda, b_w_out, final_norm):
    n_dec, dec_seq, d = x_sample.shape
    n_meta = meta_tokens.shape[0]
    assert n_meta == dec_seq, "the meta prefix is run as one more stream of the short pass"

    w = dict(
        ffn_norm=ffn_norm, mix_norm=mix_norm, final_norm=final_norm,
        ffn_w_in=ffn_w_in.astype(BF16), ffn_w_out=ffn_w_out.astype(BF16),
        a_w_in=a_w_in.astype(BF16), a_lb=a_lb, a_onorm=a_onorm, a_w_out=a_w_out.astype(BF16),
        b_w_in=b_w_in.astype(BF16), b_conv_w=b_conv_w, b_conv_b=b_conv_b,
        b_wax=jnp.concatenate([b_wa, b_wx], axis=-1).astype(BF16),
        b_ba=b_ba, b_bx=b_bx, b_lambda=b_lambda, b_w_out=b_w_out.astype(BF16),
    )

    def with_zero_stream(s):
        return jnp.concatenate([s, jnp.zeros_like(s[:, :1])], axis=1)

    xs = jnp.concatenate([x_sample, meta_tokens.astype(x_sample.dtype)[None]], axis=0)
    n_short = n_dec + 1
    ys, s_s, h_s, c_s = _trunk(
        xs, with_zero_stream(state_hgrn), with_zero_stream(state_rglru)[:, :, None, :],
        with_zero_stream(state_conv), None, w,
        hgrn_tiles=dict(bb=n_short, gs=n_short, tt=dec_seq, chunk=dec_seq),
        rglru_tiles=dict(bb=n_short, gs=n_short, tt=dec_seq),
        tm=n_short * dec_seq)

    yp, s_p, h_p, c_p = _trunk(
        x_prompt, s_s, h_s, c_s, n_dec, w,
        hgrn_tiles=dict(bb=1, gs=1, tt=256, chunk=32),
        rglru_tiles=dict(bb=2, gs=1, tt=256),
        tm=512)

    return (yp, ys[:n_dec],
            jnp.stack(s_p), jnp.stack([s[:n_dec] for s in s_s]),
            jnp.stack([h[:, 0] for h in h_p]), jnp.stack([h[:n_dec, 0] for h in h_s]),
            jnp.stack(c_p), jnp.stack([c[:n_dec] for c in c_s]))
```

```python
import functools
import itertools

import jax
import jax.numpy as jnp
from jax import lax
from jax.experimental import pallas as pl
from jax.experimental.pallas import tpu as pltpu

F32 = jnp.float32
BF16 = jnp.bfloat16

EPS = 1e-6
FFN_RES = 0.5
RG_C = 8.0
N_MIXERS = 2
HG_HEADS = 8
RG_BLOCKS = 8
CONV_W = 4

V7X_VMEM_BYTES = 64 * 1024 * 1024
VMEM_LIMIT_BYTES = V7X_VMEM_BYTES - 8 * 1024 * 1024
SUBLANES = 8
LANES = 128
V7X_MXU_DIM = 256


def _rms(x, g):
    return x * lax.rsqrt(jnp.mean(x * x, axis=-1, keepdims=True) + EPS) * g


def _dot(a, b):
    return jnp.dot(a, b, preferred_element_type=F32)


def _dot_nt(a, b):
    return lax.dot_general(a, b, (((1,), (1,)), ((), ())), preferred_element_type=F32)


def _dot_tn(a, b):
    return lax.dot_general(a, b, (((0,), (0,)), ((), ())), preferred_element_type=F32)


def _const_spec(shape):
    zeros = (0,) * len(shape)
    return pl.BlockSpec(shape, lambda *_: zeros)


def _pick_spec(shape, lead):
    tail = shape[len(lead):]
    index = tuple(lead) + (0,) * len(tail)
    return pl.BlockSpec((None,) * len(lead) + tuple(tail), lambda *_: index)


def _for_each(count, body):
    if count == 1:
        body(0)
    else:
        lax.fori_loop(0, count, lambda i, c: (body(i), c)[1], 0)


def _run_interleaved(chains):
    for _ in itertools.zip_longest(*chains):
        pass


def _ffn_kernel(x_ref, g_ref, win_ref, wout_ref, fg_ref, o_ref, *, d_ff, fc, parts, final_norm):
    rows = x_ref.shape[0] // parts

    def chain(part):
        sl = slice(part * rows, (part + 1) * rows)
        x = x_ref[sl, :]
        yb = _rms(x, g_ref[...]).astype(BF16)
        yield
        acc = None
        for lo in range(0, d_ff, fc):
            a = _dot(yb, win_ref[:, lo:lo + fc])
            b = _dot(yb, win_ref[:, d_ff + lo:d_ff + lo + fc])
            h = (a * jax.nn.sigmoid(a) * b).astype(BF16)
            p = _dot(h, wout_ref[lo:lo + fc, :])
            acc = p if acc is None else acc + p
            yield
        out = x + FFN_RES * acc
        if final_norm:
            out = _rms(out, fg_ref[...])
        o_ref[sl, :] = out

    chains = [itertools.chain([None] * part, chain(part)) for part in range(parts)]
    _run_interleaved(chains)


def _ffn(x, g, w_in, w_out, lead, fg, *, tm, parts, final_norm):
    n, d = x.shape
    d_ff = w_out.shape[-2]
    fc = V7X_MXU_DIM
    assert d_ff % fc == 0
    kern = functools.partial(_ffn_kernel, d_ff=d_ff, fc=fc, parts=parts, final_norm=final_norm)
    return pl.pallas_call(
        kern,
        grid=(n // tm,),
        in_specs=[
            pl.BlockSpec((tm, d), lambda i: (i, 0)),
            _const_spec((1, d)),
            _pick_spec(w_in.shape, lead),
            _pick_spec(w_out.shape, lead),
            _const_spec((1, d)),
        ],
        out_specs=pl.BlockSpec((tm, d), lambda i: (i, 0)),
        out_shape=jax.ShapeDtypeStruct((n, d), F32),
        compiler_params=pltpu.CompilerParams(
            dimension_semantics=("arbitrary",), vmem_limit_bytes=VMEM_LIMIT_BYTES),
        name="ffn",
    )(x, g.reshape(1, d), w_in, w_out, fg.reshape(1, d))


def _hgrn_kernel(x_ref, s0_ref, g_ref, win_ref, alb_ref, on_ref, wout_ref,
                 o_ref, s_ref,
                 st_ref, qm_scr, km_scr, qd_scr, ke_scr, v_scr, dec_scr, on_scr, tri_scr,
                 *, bb, gs, tt, chunk, layer_j, dk, dv, shared_init):
    t = pl.program_id(1)
    nt = pl.num_programs(1)
    gn = gs * tt
    hdim = HG_HEADS * dk
    vdim = HG_HEADS * dv
    cps = tt // chunk
    half = chunk // 2
    sup = LANES if tt % LANES == 0 else tt

    @pl.when(t == 0)
    def _():
        for s in range(bb):
            for h in range(HG_HEADS):
                st_ref[s, h] = s0_ref[0 if shared_init else s, h].T
        row = lax.broadcasted_iota(jnp.int32, (gn, gn), 0)
        col = lax.broadcasted_iota(jnp.int32, (gn, gn), 1)
        tri_scr[...] = jnp.where((row // chunk == col // chunk) & (col <= row), 1.0, 0.0).astype(BF16)

    alb = alb_ref[...]
    e = jnp.exp(alb - jnp.max(alb, axis=0, keepdims=True))
    lb = jnp.sum(e[:layer_j + 1], axis=0, keepdims=True) / jnp.sum(e, axis=0, keepdims=True)

    srow = lax.broadcasted_iota(jnp.int32, (sup, sup), 0)
    scol = lax.broadcasted_iota(jnp.int32, (sup, sup), 1)
    diag_causal = (srow // chunk == scol // chunk) & (scol <= srow)

    def recurrence(s):
        base = s * tt
        heads = range(HG_HEADS)
        kls = [slice(h * dk, (h + 1) * dk) for h in heads]
        vls = [slice(h * dv, (h + 1) * dv) for h in heads]
        sup_rows = [pl.ds(pl.multiple_of(base + p * sup, chunk), sup) for p in range(tt // sup)]
        chunk_rows = [pl.ds(pl.multiple_of(base + c * chunk, chunk), chunk) for c in range(cps)]
        sc = [[_dot_nt(qm_scr[r, kls[h]], km_scr[r, kls[h]]) for r in sup_rows] for h in heads]
        upd = [[_dot_tn(v_scr[r, vls[h]], ke_scr[r, kls[h]]) for r in chunk_rows] for h in heads]
        intra = [[_dot(jnp.where(diag_causal, sc[h][p], 0.0).astype(BF16), v_scr[r, vls[h]])
                  for p, r in enumerate(sup_rows)] for h in heads]
        states = []
        for h in heads:
            st = st_ref[s, h]
            at_chunk_start = []
            for c in range(cps):
                at_chunk_start.append(st.astype(BF16))
                st = st * dec_scr[s * cps + c, :, kls[h]] + upd[h][c]
            st_ref[s, h] = st
            states.append(at_chunk_start)
        for h in heads:
            for c, r in enumerate(chunk_rows):
                p, off = divmod(c * chunk, sup)
                o = intra[h][p][off:off + chunk] + _dot_nt(qd_scr[r, kls[h]], states[h][c])
                on_scr[r, vls[h]] = o * lax.rsqrt(jnp.mean(o * o, axis=-1, keepdims=True) + EPS)

    def chain(grp):
        r0 = grp * gn
        x = x_ref[grp * gs:(grp + 1) * gs].reshape(gn, x_ref.shape[-1])
        hn = _rms(x, g_ref[...]).astype(BF16)
        proj = _dot(hn, win_ref[...])
        fr = proj[:, hdim:2 * hdim]
        gate = proj[:, 2 * hdim + vdim:]
        v_scr[r0:r0 + gn, :] = proj[:, 2 * hdim:2 * hdim + vdim].astype(BF16)
        yield

        f = lb + (1.0 - lb) * jax.nn.sigmoid(fr)
        logf = jnp.log(f)
        qs = proj[:, :hdim]
        q = qs * jax.nn.sigmoid(qs)
        k = 1.0 - f
        hi = logf.astype(BF16)
        lo = (logf - hi.astype(F32)).astype(BF16)
        tri = tri_scr[...]
        b = _dot(tri, hi) + _dot(tri, lo)
        for c in range(gn // chunk):
            sl = slice(c * chunk, (c + 1) * chunk)
            dst = slice(r0 + c * chunk, r0 + (c + 1) * chunk)
            bc = b[sl]
            mid = bc[half - 1:half, :]
            last = bc[chunk - 1:chunk, :]
            qm_scr[dst, :] = (q[sl] * jnp.exp(bc - mid)).astype(BF16)
            km_scr[dst, :] = (k[sl] * jnp.exp(mid - bc)).astype(BF16)
            qd_scr[dst, :] = (q[sl] * jnp.exp(bc)).astype(BF16)
            ke_scr[dst, :] = (k[sl] * jnp.exp(last - bc)).astype(BF16)
            dec_scr[r0 // chunk + c] = jnp.exp(last)
        yield

        _for_each(gs, lambda i: recurrence(grp * gs + i))
        yield

        y = on_scr[r0:r0 + gn, :] * on_ref[...] * (gate * jax.nn.sigmoid(gate))
        o_ref[grp * gs:(grp + 1) * gs] = (x + _dot(y.astype(BF16), wout_ref[...])).reshape(gs, tt, x.shape[-1])

    _run_interleaved([chain(grp) for grp in range(bb // gs)])

    @pl.when(t == nt - 1)
    def _():
        for s in range(bb):
            for h in range(HG_HEADS):
                s_ref[s, h] = st_ref[s, h].T


def _hgrn(x, s0, shared_init, g, w_in, a_lb, onorm, w_out, lead, *, bb, gs, tt, chunk, layer_j):
    bsz, tlen, d = x.shape
    _, heads, dk, dv = s0.shape
    n = bb * tt
    shared = shared_init is not None
    kern = functools.partial(_hgrn_kernel, bb=bb, gs=gs, tt=tt, chunk=chunk, layer_j=layer_j, dk=dk, dv=dv,
                             shared_init=shared)
    if shared:
        s0_spec = pl.BlockSpec((1, heads, dk, dv), lambda b, t: (shared_init, 0, 0, 0))
    else:
        s0_spec = pl.BlockSpec((bb, heads, dk, dv), lambda b, t: (b, 0, 0, 0))
    return pl.pallas_call(
        kern,
        grid=(bsz // bb, tlen // tt),
        in_specs=[
            pl.BlockSpec((bb, tt, d), lambda b, t: (b, t, 0)),
            s0_spec,
            _const_spec((1, d)),
            _pick_spec(w_in.shape, lead),
            _const_spec(a_lb.shape),
            _const_spec((1, heads * dv)),
            _pick_spec(w_out.shape, lead),
        ],
        out_specs=[
            pl.BlockSpec((bb, tt, d), lambda b, t: (b, t, 0)),
            pl.BlockSpec((bb, heads, dk, dv), lambda b, t: (b, 0, 0, 0)),
        ],
        out_shape=[
            jax.ShapeDtypeStruct((bsz, tlen, d), F32),
            jax.ShapeDtypeStruct((bsz, heads, dk, dv), F32),
        ],
        scratch_shapes=[
            pltpu.VMEM((bb, heads, dv, dk), F32),
            pltpu.VMEM((n, heads * dk), BF16),
            pltpu.VMEM((n, heads * dk), BF16),
            pltpu.VMEM((n, heads * dk), BF16),
            pltpu.VMEM((n, heads * dk), BF16),
            pltpu.VMEM((n, heads * dv), BF16),
            pltpu.VMEM((n // chunk, 1, heads * dk), F32),
            pltpu.VMEM((n, heads * dv), F32),
            pltpu.VMEM((gs * tt, gs * tt), BF16),
        ],
        compiler_params=pltpu.CompilerParams(
            dimension_semantics=("arbitrary", "arbitrary"), vmem_limit_bytes=VMEM_LIMIT_BYTES),
        name="hgrn2",
    )(x, s0, g.reshape(1, d), w_in, a_lb, onorm.reshape(1, heads * dv), w_out)


def _segment_perm(n, tt, transpose):
    seg = tt // SUBLANES
    r = lax.broadcasted_iota(jnp.int32, (n, n), 1 if transpose else 0)
    c = lax.broadcasted_iota(jnp.int32, (n, n), 0 if transpose else 1)
    loc = r % tt
    src = (r - loc) + (loc % SUBLANES) * seg + loc // SUBLANES
    return jnp.where(c == src, 1.0, 0.0).astype(BF16)


def _rglru_kernel(x_ref, h0_ref, c0_ref, g_ref, win_ref, cw_ref, cb_ref, wax_ref,
                  ba_ref, bx_ref, lam_ref, wout_ref,
                  o_ref, h_ref, c_ref,
                  xb_scr, hs_scr, to_seg_scr, from_seg_scr,
                  *, bb, gs, tt, d_rnn, shared_init):
    t = pl.program_id(1)
    gn = gs * tt
    bw = d_rnn // RG_BLOCKS
    tail = CONV_W - 1
    seg = tt // SUBLANES

    @pl.when(t == 0)
    def _():
        for s in range(bb):
            h_ref[s] = h0_ref[0 if shared_init else s]
            c_ref[s] = c0_ref[0 if shared_init else s]
        to_seg_scr[...] = _segment_perm(gn, tt, False)
        from_seg_scr[...] = _segment_perm(gn, tt, True)

    cw = cw_ref[...]
    softplus_neg_lam = jax.nn.softplus(-lam_ref[...])
    sub = lax.broadcasted_iota(jnp.int32, (SUBLANES, d_rnn), 0)

    def stream_pieces(s):
        base = s * tt
        xg = [xb_scr[pl.ds(pl.multiple_of(base + g * SUBLANES, SUBLANES), SUBLANES), :] for g in range(seg)]
        prev = c_ref[s]
        h0 = h_ref[s]

        def delayed(g, j):
            gg, wraps = g - j, 0
            while gg < 0:
                gg, wraps = gg + seg, wraps + 1
            v = xg[gg]
            if wraps:
                v = pltpu.roll(v, wraps, 0)
                for sl in range(wraps):
                    i = tail + sl * seg + g - j
                    v = jnp.where(sub == sl, prev[i:i + 1, :], v)
            return v

        conv = []
        for g in range(seg):
            acc = cb_ref[...] + delayed(g, tail) * cw[0:1, :]
            for j in range(1, CONV_W):
                acc = acc + delayed(g, tail - j) * cw[j:j + 1, :]
            conv.append(acc)
        for i in range(tail):
            step = tt - tail + i
            c_ref[s, i:i + 1, :] = xg[step % seg][step // seg:step // seg + 1, :]
        conv = jnp.concatenate(conv, axis=0)
        yield

        cfb = conv.astype(BF16)
        pre = [_dot(cfb[:, i * bw:(i + 1) * bw], wax_ref[i]) for i in range(RG_BLOCKS)]
        r = jax.nn.sigmoid(jnp.concatenate([p[:, :bw] for p in pre], axis=-1) + ba_ref[...])
        ig = jax.nn.sigmoid(jnp.concatenate([p[:, bw:] for p in pre], axis=-1) + bx_ref[...])
        yield
        log_a = -RG_C * r * softplus_neg_lam
        a = jnp.exp(log_a)
        u = jnp.sqrt(-jnp.tanh(log_a) * (a * a + 1.0)) * (ig * conv)
        yield

        hz = [u[0:SUBLANES]]
        az = [a[0:SUBLANES]]
        for g in range(1, seg):
            ag = a[g * SUBLANES:(g + 1) * SUBLANES]
            hz.append(ag * hz[-1] + u[g * SUBLANES:(g + 1) * SUBLANES])
            az.append(ag * az[-1])
        e_end, a_end = hz[-1], az[-1]
        d = 1
        while d < SUBLANES:
            keep = sub >= d
            e_sh = pltpu.roll(e_end, d, 0)
            a_sh = pltpu.roll(a_end, d, 0)
            e_end = jnp.where(keep, a_end * e_sh + e_end, e_end)
            a_end = jnp.where(keep, a_end * a_sh, a_end)
            d *= 2
        h_end = a_end * h0 + e_end
        h_start = jnp.where(sub == 0, h0, pltpu.roll(h_end, 1, 0))
        h_ref[s] = h_end[SUBLANES - 1:SUBLANES, :]
        for g in range(seg):
            rows = pl.ds(pl.multiple_of(base + g * SUBLANES, SUBLANES), SUBLANES)
            hs_scr[rows, :] = hz[g] + az[g] * h_start

    def chain(grp):
        r0 = grp * gn
        x = x_ref[grp * gs:(grp + 1) * gs].reshape(gn, x_ref.shape[-1])
        hn = _rms(x, g_ref[...]).astype(BF16)
        hp = _dot(to_seg_scr[...], hn).astype(BF16)
        yield
        xb_scr[r0:r0 + gn, :] = _dot(hp, win_ref[:, :d_rnn])
        yield
        gb = _dot(hp, win_ref[:, d_rnn:])
        yield

        if gs == 1:
            yield from stream_pieces(grp)
        else:
            _for_each(gs, lambda i: list(stream_pieces(grp * gs + i)))
        yield

        y = (hs_scr[r0:r0 + gn, :] * jax.nn.gelu(gb)).astype(BF16)
        yield
        yn = _dot(from_seg_scr[...], y).astype(BF16)
        o_ref[grp * gs:(grp + 1) * gs] = (x + _dot(yn, wout_ref[...])).reshape(gs, tt, x.shape[-1])

    _run_interleaved([chain(grp) for grp in range(bb // gs)])


def _rglru(x, h0, c0, shared_init, g, w_in, conv_w, conv_b, wax, ba, bx, lam, w_out, lead, *, bb, gs, tt):
    bsz, tlen, d = x.shape
    d_rnn = h0.shape[-1]
    tail = c0.shape[1]
    n = bb * tt
    shared = shared_init is not None
    kern = functools.partial(_rglru_kernel, bb=bb, gs=gs, tt=tt, d_rnn=d_rnn, shared_init=shared)
    row = lambda v: v.reshape(1, d_rnn)
    if shared:
        init_map, init_n = (lambda b, t: (shared_init, 0, 0)), 1
    else:
        init_map, init_n = (lambda b, t: (b, 0, 0)), bb
    return pl.pallas_call(
        kern,
        grid=(bsz // bb, tlen // tt),
        in_specs=[
            pl.BlockSpec((bb, tt, d), lambda b, t: (b, t, 0)),
            pl.BlockSpec((init_n, 1, d_rnn), init_map),
            pl.BlockSpec((init_n, tail, d_rnn), init_map),
            _const_spec((1, d)),
            _pick_spec(w_in.shape, lead),
            _const_spec(conv_w.shape),
            _const_spec((1, d_rnn)),
            _pick_spec(wax.shape, lead),
            _const_spec((1, d_rnn)),
            _const_spec((1, d_rnn)),
            _const_spec((1, d_rnn)),
            _pick_spec(w_out.shape, lead),
        ],
        out_specs=[
            pl.BlockSpec((bb, tt, d), lambda b, t: (b, t, 0)),
            pl.BlockSpec((bb, 1, d_rnn), lambda b, t: (b, 0, 0)),
            pl.BlockSpec((bb, tail, d_rnn), lambda b, t: (b, 0, 0)),
        ],
        out_shape=[
            jax.ShapeDtypeStruct((bsz, tlen, d), F32),
            jax.ShapeDtypeStruct((bsz, 1, d_rnn), F32),
            jax.ShapeDtypeStruct((bsz, tail, d_rnn), F32),
        ],
        scratch_shapes=[
            pltpu.VMEM((n, d_rnn), F32),
            pltpu.VMEM((n, d_rnn), F32),
            pltpu.VMEM((gs * tt, gs * tt), BF16),
            pltpu.VMEM((gs * tt, gs * tt), BF16),
        ],
        compiler_params=pltpu.CompilerParams(
            dimension_semantics=("arbitrary", "arbitrary"), vmem_limit_bytes=VMEM_LIMIT_BYTES),
        name="rglru",
    )(x, h0, c0, g.reshape(1, d), w_in, conv_w, row(conv_b), wax, row(ba), row(bx), row(lam), w_out)


def _trunk(x, s_hgrn, s_h, s_conv, shared_init, w, *, hgrn_tiles, rglru_tiles, ffn_tiles):
    bsz, tlen, d = x.shape
    depth = w["ffn_norm"].shape[0]
    new_s, new_h, new_c = [], [], []
    h = x.reshape(bsz * tlen, d)
    for layer in range(depth):
        j = layer // N_MIXERS
        h = _ffn(h, w["ffn_norm"][layer, 0], w["ffn_w_in"], w["ffn_w_out"], (layer, 0),
                 w["final_norm"], final_norm=False, **ffn_tiles)
        h3 = h.reshape(bsz, tlen, d)
        if layer % N_MIXERS == 0:
            h3, s_new = _hgrn(h3, s_hgrn[j], shared_init, w["mix_norm"][layer], w["a_w_in"], w["a_lb"],
                              w["a_onorm"][j], w["a_w_out"], (j,), layer_j=j, **hgrn_tiles)
            new_s.append(s_new)
        else:
            h3, h_new, c_new = _rglru(h3, s_h[j], s_conv[j], shared_init, w["mix_norm"][layer], w["b_w_in"],
                                      w["b_conv_w"][j], w["b_conv_b"][j], w["b_wax"], w["b_ba"][j],
                                      w["b_bx"][j], w["b_lambda"][j], w["b_w_out"], (j,), **rglru_tiles)
            new_h.append(h_new)
            new_c.append(c_new)
        h = h3.reshape(bsz * tlen, d)
        h = _ffn(h, w["ffn_norm"][layer, 1], w["ffn_w_in"], w["ffn_w_out"], (layer, 1),
                 w["final_norm"], final_norm=(layer == depth - 1), **ffn_tiles)
    return h.reshape(bsz, tlen, d), new_s, new_h, new_c


def kernel(x_prompt, x_sample, state_hgrn, state_rglru, state_conv, meta_tokens, ffn_norm, ffn_w_in, ffn_w_out, mix_norm, a_w_in, a_lb, a_onorm, a_w_out, b_w_in, b_conv_w, b_conv_b, b_wa, b_ba, b_wx, b_bx, b_lambda, b_w_out, final_norm):
    n_dec, dec_seq, d = x_sample.shape
    n_meta = meta_tokens.shape[0]
    assert n_meta == dec_seq, "the meta prefix is run as one more stream of the short pass"

    w = dict(
        ffn_norm=ffn_norm, mix_norm=mix_norm, final_norm=final_norm,
        ffn_w_in=ffn_w_in.astype(BF16), ffn_w_out=ffn_w_out.astype(BF16),
        a_w_in=a_w_in.astype(BF16), a_lb=a_lb, a_onorm=a_onorm, a_w_out=a_w_out.astype(BF16),
        b_w_in=b_w_in.astype(BF16), b_conv_w=b_conv_w, b_conv_b=b_conv_b,
        b_wax=jnp.concatenate([b_wa, b_wx], axis=-1).astype(BF16),
        b_ba=b_ba, b_bx=b_bx, b_lambda=b_lambda, b_w_out=b_w_out.astype(BF16),
    )

    def with_zero_stream(s):
        return jnp.concatenate([s, jnp.zeros_like(s[:, :1])], axis=1)

    xs = jnp.concatenate([x_sample, meta_tokens.astype(x_sample.dtype)[None]], axis=0)
    n_short = n_dec + 1
    ys, s_s, h_s, c_s = _trunk(
        xs, with_zero_stream(state_hgrn), with_zero_stream(state_rglru)[:, :, None, :],
        with_zero_stream(state_conv), None, w,
        hgrn_tiles=dict(bb=n_short, gs=n_short, tt=dec_seq, chunk=dec_seq),
        rglru_tiles=dict(bb=n_short, gs=n_short, tt=dec_seq),
        ffn_tiles=dict(tm=n_short * dec_seq, parts=1))

    yp, s_p, h_p, c_p = _trunk(
        x_prompt, s_s, h_s, c_s, n_dec, w,
        hgrn_tiles=dict(bb=2, gs=1, tt=256, chunk=32),
        rglru_tiles=dict(bb=2, gs=1, tt=256),
        ffn_tiles=dict(tm=1024, parts=2))

    return (yp, ys[:n_dec],
            jnp.stack(s_p), jnp.stack([s[:n_dec] for s in s_s]),
            jnp.stack([h[:, 0] for h in h_p]), jnp.stack([h[:n_dec, 0] for h in h_s]),
            jnp.stack(c_p), jnp.stack([c[:n_dec] for c in c_s]))
```

```python
import functools
import itertools

import jax
import jax.numpy as jnp
from jax import lax
from jax.experimental import pallas as pl
from jax.experimental.pallas import tpu as pltpu

F32 = jnp.float32
BF16 = jnp.bfloat16

EPS = 1e-6
FFN_RES = 0.5
RG_C = 8.0
N_MIXERS = 2
HG_HEADS = 8
RG_BLOCKS = 8
CONV_W = 4

V7X_VMEM_BYTES = 64 * 1024 * 1024
VMEM_LIMIT_BYTES = V7X_VMEM_BYTES - 8 * 1024 * 1024
SUBLANES = 8
LANES = 128
V7X_MXU_DIM = 256


def _rms(x, g):
    return x * lax.rsqrt(jnp.mean(x * x, axis=-1, keepdims=True) + EPS) * g


def _dot(a, b):
    return jnp.dot(a, b, preferred_element_type=F32)


def _dot_nt(a, b):
    return lax.dot_general(a, b, (((1,), (1,)), ((), ())), preferred_element_type=F32)


def _dot_tn(a, b):
    return lax.dot_general(a, b, (((0,), (0,)), ((), ())), preferred_element_type=F32)


def _const_spec(shape):
    zeros = (0,) * len(shape)
    return pl.BlockSpec(shape, lambda *_: zeros)


def _pick_spec(shape, lead):
    tail = shape[len(lead):]
    index = tuple(lead) + (0,) * len(tail)
    return pl.BlockSpec((None,) * len(lead) + tuple(tail), lambda *_: index)


def _for_each(count, body):
    if count == 1:
        body(0)
    else:
        lax.fori_loop(0, count, lambda i, c: (body(i), c)[1], 0)


def _run_interleaved(chains):
    for _ in itertools.zip_longest(*chains):
        pass


def _ffn_kernel(x_ref, g_ref, win_ref, wout_ref, fg_ref, o_ref, *, d_ff, fc, parts, final_norm):
    rows = x_ref.shape[0] // parts

    def chain(part):
        sl = slice(part * rows, (part + 1) * rows)
        x = x_ref[sl, :]
        yb = _rms(x, g_ref[...]).astype(BF16)
        yield
        acc = None
        for lo in range(0, d_ff, fc):
            a = _dot(yb, win_ref[:, lo:lo + fc])
            b = _dot(yb, win_ref[:, d_ff + lo:d_ff + lo + fc])
            h = (a * jax.nn.sigmoid(a) * b).astype(BF16)
            p = _dot(h, wout_ref[lo:lo + fc, :])
            acc = p if acc is None else acc + p
            yield
        out = x + FFN_RES * acc
        if final_norm:
            out = _rms(out, fg_ref[...])
        o_ref[sl, :] = out

    chains = [itertools.chain([None] * part, chain(part)) for part in range(parts)]
    _run_interleaved(chains)


def _ffn(x, g, w_in, w_out, lead, fg, *, tm, parts, final_norm):
    n, d = x.shape
    d_ff = w_out.shape[-2]
    fc = V7X_MXU_DIM
    assert d_ff % fc == 0
    kern = functools.partial(_ffn_kernel, d_ff=d_ff, fc=fc, parts=parts, final_norm=final_norm)
    return pl.pallas_call(
        kern,
        grid=(n // tm,),
        in_specs=[
            pl.BlockSpec((tm, d), lambda i: (i, 0)),
            _const_spec((1, d)),
            _pick_spec(w_in.shape, lead),
            _pick_spec(w_out.shape, lead),
            _const_spec((1, d)),
        ],
        out_specs=pl.BlockSpec((tm, d), lambda i: (i, 0)),
        out_shape=jax.ShapeDtypeStruct((n, d), F32),
        compiler_params=pltpu.CompilerParams(
            dimension_semantics=("arbitrary",), vmem_limit_bytes=VMEM_LIMIT_BYTES),
        name="ffn",
    )(x, g.reshape(1, d), w_in, w_out, fg.reshape(1, d))


def _hgrn_kernel(x_ref, s0_ref, g_ref, win_ref, alb_ref, on_ref, wout_ref,
                 o_ref, s_ref,
                 st_ref, qm_scr, km_scr, qd_scr, ke_scr, qn_scr, kn_scr, v_scr, dec_scr, on_scr, tri_scr,
                 *, bb, gs, tt, chunk, pair, layer_j, dk, dv, shared_init):
    t = pl.program_id(1)
    nt = pl.num_programs(1)
    gn = gs * tt
    hdim = HG_HEADS * dk
    vdim = HG_HEADS * dv
    half = chunk // 2
    span = chunk * pair
    sps = tt // span
    sup = LANES if tt % LANES == 0 else tt
    assert pair in (1, 2) and sup % span == 0

    @pl.when(t == 0)
    def _():
        for s in range(bb):
            for h in range(HG_HEADS):
                st_ref[s, h] = s0_ref[0 if shared_init else s, h].T
        row = lax.broadcasted_iota(jnp.int32, (gn, gn), 0)
        col = lax.broadcasted_iota(jnp.int32, (gn, gn), 1)
        tri_scr[...] = jnp.where((row // chunk == col // chunk) & (col <= row), 1.0, 0.0).astype(BF16)

    alb = alb_ref[...]
    e = jnp.exp(alb - jnp.max(alb, axis=0, keepdims=True))
    lb = jnp.sum(e[:layer_j + 1], axis=0, keepdims=True) / jnp.sum(e, axis=0, keepdims=True)

    srow = lax.broadcasted_iota(jnp.int32, (sup, sup), 0)
    scol = lax.broadcasted_iota(jnp.int32, (sup, sup), 1)
    diag_causal = (srow // chunk == scol // chunk) & (scol <= srow)
    below_diag = (srow // chunk == scol // chunk + 1) & (srow // span == scol // span)

    def recurrence(s):
        base = s * tt
        heads = range(HG_HEADS)
        kls = [slice(h * dk, (h + 1) * dk) for h in heads]
        vls = [slice(h * dv, (h + 1) * dv) for h in heads]
        sup_rows = [pl.ds(pl.multiple_of(base + p * sup, chunk), sup) for p in range(tt // sup)]
        span_rows = [pl.ds(pl.multiple_of(base + c * span, chunk), span) for c in range(sps)]
        sc = [[_dot_nt(qm_scr[r, kls[h]], km_scr[r, kls[h]]) for r in sup_rows] for h in heads]
        if pair == 2:
            nb = [[_dot_nt(qn_scr[r, kls[h]], kn_scr[r, kls[h]]) for r in sup_rows] for h in heads]
            sc = [[jnp.where(below_diag, nb[h][p], sc[h][p]) for p in range(len(sup_rows))] for h in heads]
            wanted = diag_causal | below_diag
        else:
            wanted = diag_causal
        upd = [[_dot_tn(v_scr[r, vls[h]], ke_scr[r, kls[h]]) for r in span_rows] for h in heads]
        intra = [[_dot(jnp.where(wanted, sc[h][p], 0.0).astype(BF16), v_scr[r, vls[h]])
                  for p, r in enumerate(sup_rows)] for h in heads]
        states = []
        for h in heads:
            st = st_ref[s, h]
            at_span_start = []
            for c in range(sps):
                at_span_start.append(st.astype(BF16))
                st = st * dec_scr[s * sps + c, :, kls[h]] + upd[h][c]
            st_ref[s, h] = st
            states.append(at_span_start)
        for h in heads:
            for c, r in enumerate(span_rows):
                p, off = divmod(c * span, sup)
                o = intra[h][p][off:off + span] + _dot_nt(qd_scr[r, kls[h]], states[h][c])
                on_scr[r, vls[h]] = o * lax.rsqrt(jnp.mean(o * o, axis=-1, keepdims=True) + EPS)

    def chain(grp):
        r0 = grp * gn
        x = x_ref[grp * gs:(grp + 1) * gs].reshape(gn, x_ref.shape[-1])
        hn = _rms(x, g_ref[...]).astype(BF16)
        proj = _dot(hn, win_ref[...])
        fr = proj[:, hdim:2 * hdim]
        gate = proj[:, 2 * hdim + vdim:]
        v_scr[r0:r0 + gn, :] = proj[:, 2 * hdim:2 * hdim + vdim].astype(BF16)
        yield

        f = lb + (1.0 - lb) * jax.nn.sigmoid(fr)
        logf = jnp.log(f)
        qs = proj[:, :hdim]
        q = qs * jax.nn.sigmoid(qs)
        k = 1.0 - f
        hi = logf.astype(BF16)
        lo = (logf - hi.astype(F32)).astype(BF16)
        tri = tri_scr[...]
        b = _dot(tri, hi) + _dot(tri, lo)
        for c2 in range(gn // span):
            parts = []
            for c in range(c2 * pair, (c2 + 1) * pair):
                sl = slice(c * chunk, (c + 1) * chunk)
                dst = slice(r0 + c * chunk, r0 + (c + 1) * chunk)
                bc = b[sl]
                mid = bc[half - 1:half, :]
                last = bc[chunk - 1:chunk, :]
                q_mid = q[sl] * jnp.exp(bc - mid)
                k_mid = k[sl] * jnp.exp(mid - bc)
                qm_scr[dst, :] = q_mid.astype(BF16)
                km_scr[dst, :] = k_mid.astype(BF16)
                parts.append((dst, q_mid * jnp.exp(mid), k_mid * jnp.exp(last - mid), jnp.exp(last)))
            if pair == 1:
                (dst, q_in, k_out, dec), = parts
                qd_scr[dst, :] = q_in.astype(BF16)
                ke_scr[dst, :] = k_out.astype(BF16)
            else:
                (dst_a, q_in_a, k_out_a, dec_a), (dst_b, q_in_b, k_out_b, dec_b) = parts
                qn_scr[dst_a, :] = q_in_a.astype(BF16)
                qn_scr[dst_b, :] = q_in_b.astype(BF16)
                kn_scr[dst_a, :] = k_out_a.astype(BF16)
                kn_scr[dst_b, :] = k_out_b.astype(BF16)
                qd_scr[dst_a, :] = q_in_a.astype(BF16)
                qd_scr[dst_b, :] = (q_in_b * dec_a).astype(BF16)
                ke_scr[dst_a, :] = (k_out_a * dec_b).astype(BF16)
                ke_scr[dst_b, :] = k_out_b.astype(BF16)
                dec = dec_a * dec_b
            dec_scr[r0 // span + c2] = dec
        yield

        _for_each(gs, lambda i: recurrence(grp * gs + i))
        yield

        y = on_scr[r0:r0 + gn, :] * on_ref[...] * (gate * jax.nn.sigmoid(gate))
        o_ref[grp * gs:(grp + 1) * gs] = (x + _dot(y.astype(BF16), wout_ref[...])).reshape(gs, tt, x.shape[-1])

    _run_interleaved([chain(grp) for grp in range(bb // gs)])

    @pl.when(t == nt - 1)
    def _():
        for s in range(bb):
            for h in range(HG_HEADS):
                s_ref[s, h] = st_ref[s, h].T


def _hgrn(x, s0, shared_init, g, w_in, a_lb, onorm, w_out, lead, *, bb, gs, tt, chunk, pair, layer_j):
    bsz, tlen, d = x.shape
    _, heads, dk, dv = s0.shape
    n = bb * tt
    shared = shared_init is not None
    kern = functools.partial(_hgrn_kernel, bb=bb, gs=gs, tt=tt, chunk=chunk, pair=pair, layer_j=layer_j, dk=dk, dv=dv,
                             shared_init=shared)
    if shared:
        s0_spec = pl.BlockSpec((1, heads, dk, dv), lambda b, t: (shared_init, 0, 0, 0))
    else:
        s0_spec = pl.BlockSpec((bb, heads, dk, dv), lambda b, t: (b, 0, 0, 0))
    return pl.pallas_call(
        kern,
        grid=(bsz // bb, tlen // tt),
        in_specs=[
            pl.BlockSpec((bb, tt, d), lambda b, t: (b, t, 0)),
            s0_spec,
            _const_spec((1, d)),
            _pick_spec(w_in.shape, lead),
            _const_spec(a_lb.shape),
            _const_spec((1, heads * dv)),
            _pick_spec(w_out.shape, lead),
        ],
        out_specs=[
            pl.BlockSpec((bb, tt, d), lambda b, t: (b, t, 0)),
            pl.BlockSpec((bb, heads, dk, dv), lambda b, t: (b, 0, 0, 0)),
        ],
        out_shape=[
            jax.ShapeDtypeStruct((bsz, tlen, d), F32),
            jax.ShapeDtypeStruct((bsz, heads, dk, dv), F32),
        ],
        scratch_shapes=[
            pltpu.VMEM((bb, heads, dv, dk), F32),
            pltpu.VMEM((n, heads * dk), BF16),
            pltpu.VMEM((n, heads * dk), BF16),
            pltpu.VMEM((n, heads * dk), BF16),
            pltpu.VMEM((n, heads * dk), BF16),
            pltpu.VMEM((n, heads * dk) if pair == 2 else (SUBLANES, LANES), BF16),
            pltpu.VMEM((n, heads * dk) if pair == 2 else (SUBLANES, LANES), BF16),
            pltpu.VMEM((n, heads * dv), BF16),
            pltpu.VMEM((n // (chunk * pair), 1, heads * dk), F32),
            pltpu.VMEM((n, heads * dv), F32),
            pltpu.VMEM((gs * tt, gs * tt), BF16),
        ],
        compiler_params=pltpu.CompilerParams(
            dimension_semantics=("arbitrary", "arbitrary"), vmem_limit_bytes=VMEM_LIMIT_BYTES),
        name="hgrn2",
    )(x, s0, g.reshape(1, d), w_in, a_lb, onorm.reshape(1, heads * dv), w_out)


def _segment_perm(n, tt, transpose):
    seg = tt // SUBLANES
    r = lax.broadcasted_iota(jnp.int32, (n, n), 1 if transpose else 0)
    c = lax.broadcasted_iota(jnp.int32, (n, n), 0 if transpose else 1)
    loc = r % tt
    src = (r - loc) + (loc % SUBLANES) * seg + loc // SUBLANES
    return jnp.where(c == src, 1.0, 0.0).astype(BF16)


def _rglru_kernel(x_ref, h0_ref, c0_ref, g_ref, win_ref, cw_ref, cb_ref, wax_ref,
                  ba_ref, bx_ref, lam_ref, wout_ref,
                  o_ref, h_ref, c_ref,
                  xb_scr, hs_scr, to_seg_scr, from_seg_scr,
                  *, bb, gs, tt, d_rnn, shared_init):
    t = pl.program_id(1)
    gn = gs * tt
    bw = d_rnn // RG_BLOCKS
    tail = CONV_W - 1
    seg = tt // SUBLANES

    @pl.when(t == 0)
    def _():
        for s in range(bb):
            h_ref[s] = h0_ref[0 if shared_init else s]
            c_ref[s] = c0_ref[0 if shared_init else s]
        to_seg_scr[...] = _segment_perm(gn, tt, False)
        from_seg_scr[...] = _segment_perm(gn, tt, True)

    cw = cw_ref[...]
    softplus_neg_lam = jax.nn.softplus(-lam_ref[...])
    sub = lax.broadcasted_iota(jnp.int32, (SUBLANES, d_rnn), 0)

    def stream_pieces(s):
        base = s * tt
        xg = [xb_scr[pl.ds(pl.multiple_of(base + g * SUBLANES, SUBLANES), SUBLANES), :] for g in range(seg)]
        prev = c_ref[s]
        h0 = h_ref[s]

        def delayed(g, j):
            gg, wraps = g - j, 0
            while gg < 0:
                gg, wraps = gg + seg, wraps + 1
            v = xg[gg]
            if wraps:
                v = pltpu.roll(v, wraps, 0)
                for sl in range(wraps):
                    i = tail + sl * seg + g - j
                    v = jnp.where(sub == sl, prev[i:i + 1, :], v)
            return v

        conv = []
        for g in range(seg):
            acc = cb_ref[...] + delayed(g, tail) * cw[0:1, :]
            for j in range(1, CONV_W):
                acc = acc + delayed(g, tail - j) * cw[j:j + 1, :]
            conv.append(acc)
        for i in range(tail):
            step = tt - tail + i
            c_ref[s, i:i + 1, :] = xg[step % seg][step // seg:step // seg + 1, :]
        conv = jnp.concatenate(conv, axis=0)
        yield

        cfb = conv.astype(BF16)
        pre = [_dot(cfb[:, i * bw:(i + 1) * bw], wax_ref[i]) for i in range(RG_BLOCKS)]
        r = jax.nn.sigmoid(jnp.concatenate([p[:, :bw] for p in pre], axis=-1) + ba_ref[...])
        ig = jax.nn.sigmoid(jnp.concatenate([p[:, bw:] for p in pre], axis=-1) + bx_ref[...])
        yield
        log_a = -RG_C * r * softplus_neg_lam
        a = jnp.exp(log_a)
        u = jnp.sqrt(-jnp.tanh(log_a) * (a * a + 1.0)) * (ig * conv)
        yield

        hz = [u[0:SUBLANES]]
        az = [a[0:SUBLANES]]
        for g in range(1, seg):
            ag = a[g * SUBLANES:(g + 1) * SUBLANES]
            hz.append(ag * hz[-1] + u[g * SUBLANES:(g + 1) * SUBLANES])
            az.append(ag * az[-1])
        e_end, a_end = hz[-1], az[-1]
        d = 1
        while d < SUBLANES:
            keep = sub >= d
            e_sh = pltpu.roll(e_end, d, 0)
            a_sh = pltpu.roll(a_end, d, 0)
            e_end = jnp.where(keep, a_end * e_sh + e_end, e_end)
            a_end = jnp.where(keep, a_end * a_sh, a_end)
            d *= 2
        h_end = a_end * h0 + e_end
        h_start = jnp.where(sub == 0, h0, pltpu.roll(h_end, 1, 0))
        h_ref[s] = h_end[SUBLANES - 1:SUBLANES, :]
        for g in range(seg):
            rows = pl.ds(pl.multiple_of(base + g * SUBLANES, SUBLANES), SUBLANES)
            hs_scr[rows, :] = hz[g] + az[g] * h_start

    def chain(grp):
        r0 = grp * gn
        x = x_ref[grp * gs:(grp + 1) * gs].reshape(gn, x_ref.shape[-1])
        hn = _rms(x, g_ref[...]).astype(BF16)
        hp = _dot(to_seg_scr[...], hn).astype(BF16)
        yield
        xb_scr[r0:r0 + gn, :] = _dot(hp, win_ref[:, :d_rnn])
        yield
        gb = _dot(hp, win_ref[:, d_rnn:])
        yield

        if gs == 1:
            yield from stream_pieces(grp)
        else:
            _for_each(gs, lambda i: list(stream_pieces(grp * gs + i)))
        yield

        y = (hs_scr[r0:r0 + gn, :] * jax.nn.gelu(gb)).astype(BF16)
        yield
        yn = _dot(from_seg_scr[...], y).astype(BF16)
        o_ref[grp * gs:(grp + 1) * gs] = (x + _dot(yn, wout_ref[...])).reshape(gs, tt, x.shape[-1])

    _run_interleaved([chain(grp) for grp in range(bb // gs)])


def _rglru(x, h0, c0, shared_init, g, w_in, conv_w, conv_b, wax, ba, bx, lam, w_out, lead, *, bb, gs, tt):
    bsz, tlen, d = x.shape
    d_rnn = h0.shape[-1]
    tail = c0.shape[1]
    n = bb * tt
    shared = shared_init is not None
    kern = functools.partial(_rglru_kernel, bb=bb, gs=gs, tt=tt, d_rnn=d_rnn, shared_init=shared)
    row = lambda v: v.reshape(1, d_rnn)
    if shared:
        init_map, init_n = (lambda b, t: (shared_init, 0, 0)), 1
    else:
        init_map, init_n = (lambda b, t: (b, 0, 0)), bb
    return pl.pallas_call(
        kern,
        grid=(bsz // bb, tlen // tt),
        in_specs=[
            pl.BlockSpec((bb, tt, d), lambda b, t: (b, t, 0)),
            pl.BlockSpec((init_n, 1, d_rnn), init_map),
            pl.BlockSpec((init_n, tail, d_rnn), init_map),
            _const_spec((1, d)),
            _pick_spec(w_in.shape, lead),
            _const_spec(conv_w.shape),
            _const_spec((1, d_rnn)),
            _pick_spec(wax.shape, lead),
            _const_spec((1, d_rnn)),
            _const_spec((1, d_rnn)),
            _const_spec((1, d_rnn)),
            _pick_spec(w_out.shape, lead),
        ],
        out_specs=[
            pl.BlockSpec((bb, tt, d), lambda b, t: (b, t, 0)),
            pl.BlockSpec((bb, 1, d_rnn), lambda b, t: (b, 0, 0)),
            pl.BlockSpec((bb, tail, d_rnn), lambda b, t: (b, 0, 0)),
        ],
        out_shape=[
            jax.ShapeDtypeStruct((bsz, tlen, d), F32),
            jax.ShapeDtypeStruct((bsz, 1, d_rnn), F32),
            jax.ShapeDtypeStruct((bsz, tail, d_rnn), F32),
        ],
        scratch_shapes=[
            pltpu.VMEM((n, d_rnn), F32),
            pltpu.VMEM((n, d_rnn), F32),
            pltpu.VMEM((gs * tt, gs * tt), BF16),
            pltpu.VMEM((gs * tt, gs * tt), BF16),
        ],
        compiler_params=pltpu.CompilerParams(
            dimension_semantics=("arbitrary", "arbitrary"), vmem_limit_bytes=VMEM_LIMIT_BYTES),
        name="rglru",
    )(x, h0, c0, g.reshape(1, d), w_in, conv_w, row(conv_b), wax, row(ba), row(bx), row(lam), w_out)


def _trunk(x, s_hgrn, s_h, s_conv, shared_init, w, *, hgrn_tiles, rglru_tiles, ffn_tiles):
    bsz, tlen, d = x.shape
    depth = w["ffn_norm"].shape[0]
    new_s, new_h, new_c = [], [], []
    h = x.reshape(bsz * tlen, d)
    for layer in range(depth):
        j = layer // N_MIXERS
        h = _ffn(h, w["ffn_norm"][layer, 0], w["ffn_w_in"], w["ffn_w_out"], (layer, 0),
                 w["final_norm"], final_norm=False, **ffn_tiles)
        h3 = h.reshape(bsz, tlen, d)
        if layer % N_MIXERS == 0:
            h3, s_new = _hgrn(h3, s_hgrn[j], shared_init, w["mix_norm"][layer], w["a_w_in"], w["a_lb"],
                              w["a_onorm"][j], w["a_w_out"], (j,), layer_j=j, **hgrn_tiles)
            new_s.append(s_new)
        else:
            h3, h_new, c_new = _rglru(h3, s_h[j], s_conv[j], shared_init, w["mix_norm"][layer], w["b_w_in"],
                                      w["b_conv_w"][j], w["b_conv_b"][j], w["b_wax"], w["b_ba"][j],
                                      w["b_bx"][j], w["b_lambda"][j], w["b_w_out"], (j,), **rglru_tiles)
            new_h.append(h_new)
            new_c.append(c_new)
        h = h3.reshape(bsz * tlen, d)
        h = _ffn(h, w["ffn_norm"][layer, 1], w["ffn_w_in"], w["ffn_w_out"], (layer, 1),
                 w["final_norm"], final_norm=(layer == depth - 1), **ffn_tiles)
    return h.reshape(bsz, tlen, d), new_s, new_h, new_c


def kernel(x_prompt, x_sample, state_hgrn, state_rglru, state_conv, meta_tokens, ffn_norm, ffn_w_in, ffn_w_out, mix_norm, a_w_in, a_lb, a_onorm, a_w_out, b_w_in, b_conv_w, b_conv_b, b_wa, b_ba, b_wx, b_bx, b_lambda, b_w_out, final_norm):
    n_dec, dec_seq, d = x_sample.shape
    n_meta = meta_tokens.shape[0]
    assert n_meta == dec_seq, "the meta prefix is run as one more stream of the short pass"

    w = dict(
        ffn_norm=ffn_norm, mix_norm=mix_norm, final_norm=final_norm,
        ffn_w_in=ffn_w_in.astype(BF16), ffn_w_out=ffn_w_out.astype(BF16),
        a_w_in=a_w_in.astype(BF16), a_lb=a_lb, a_onorm=a_onorm, a_w_out=a_w_out.astype(BF16),
        b_w_in=b_w_in.astype(BF16), b_conv_w=b_conv_w, b_conv_b=b_conv_b,
        b_wax=jnp.concatenate([b_wa, b_wx], axis=-1).astype(BF16),
        b_ba=b_ba, b_bx=b_bx, b_lambda=b_lambda, b_w_out=b_w_out.astype(BF16),
    )

    def with_zero_stream(s):
        return jnp.concatenate([s, jnp.zeros_like(s[:, :1])], axis=1)

    xs = jnp.concatenate([x_sample, meta_tokens.astype(x_sample.dtype)[None]], axis=0)
    n_short = n_dec + 1
    ys, s_s, h_s, c_s = _trunk(
        xs, with_zero_stream(state_hgrn), with_zero_stream(state_rglru)[:, :, None, :],
        with_zero_stream(state_conv), None, w,
        hgrn_tiles=dict(bb=n_short, gs=n_short, tt=dec_seq, chunk=dec_seq, pair=1),
        rglru_tiles=dict(bb=n_short, gs=n_short, tt=dec_seq),
        ffn_tiles=dict(tm=n_short * dec_seq, parts=1))

    yp, s_p, h_p, c_p = _trunk(
        x_prompt, s_s, h_s, c_s, n_dec, w,
        hgrn_tiles=dict(bb=2, gs=1, tt=256, chunk=32, pair=2),
        rglru_tiles=dict(bb=2, gs=1, tt=256),
        ffn_tiles=dict(tm=1024, parts=2))

    return (yp, ys[:n_dec],
            jnp.stack(s_p), jnp.stack([s[:n_dec] for s in s_s]),
            jnp.stack([h[:, 0] for h in h_p]), jnp.stack([h[:n_dec, 0] for h in h_s]),
            jnp.stack(c_p), jnp.stack([c[:n_dec] for c in c_s]))
```

```python
import functools
import itertools

import jax
import jax.numpy as jnp
from jax import lax
from jax.experimental import pallas as pl
from jax.experimental.pallas import tpu as pltpu

F32 = jnp.float32
BF16 = jnp.bfloat16

EPS = 1e-6
FFN_RES = 0.5
RG_C = 8.0
N_MIXERS = 2
HG_HEADS = 8
RG_BLOCKS = 8
CONV_W = 4

V7X_VMEM_BYTES = 64 * 1024 * 1024
VMEM_LIMIT_BYTES = V7X_VMEM_BYTES - 8 * 1024 * 1024
SUBLANES = 8
LANES = 128
V7X_MXU_DIM = 256


def _rms(x, g):
    return x * lax.rsqrt(jnp.mean(x * x, axis=-1, keepdims=True) + EPS) * g


def _dot(a, b):
    return jnp.dot(a, b, preferred_element_type=F32)


def _dot_nt(a, b):
    return lax.dot_general(a, b, (((1,), (1,)), ((), ())), preferred_element_type=F32)


def _dot_tn(a, b):
    return lax.dot_general(a, b, (((0,), (0,)), ((), ())), preferred_element_type=F32)


def _const_spec(shape):
    zeros = (0,) * len(shape)
    return pl.BlockSpec(shape, lambda *_: zeros)


def _pick_spec(shape, lead):
    tail = shape[len(lead):]
    index = tuple(lead) + (0,) * len(tail)
    return pl.BlockSpec((None,) * len(lead) + tuple(tail), lambda *_: index)


def _for_each(count, body):
    if count == 1:
        body(0)
    else:
        lax.fori_loop(0, count, lambda i, c: (body(i), c)[1], 0)


def _run_interleaved(chains):
    for _ in itertools.zip_longest(*chains):
        pass


def _ffn_kernel(x_ref, g_ref, win_ref, wout_ref, fg_ref, o_ref, *, d_ff, fc, parts, final_norm):
    rows = x_ref.shape[0] // parts

    def chain(part):
        sl = slice(part * rows, (part + 1) * rows)
        x = x_ref[sl, :]
        yb = _rms(x, g_ref[...]).astype(BF16)
        yield
        acc = None
        for lo in range(0, d_ff, fc):
            a = _dot(yb, win_ref[:, lo:lo + fc])
            b = _dot(yb, win_ref[:, d_ff + lo:d_ff + lo + fc])
            h = (a * jax.nn.sigmoid(a) * b).astype(BF16)
            p = _dot(h, wout_ref[lo:lo + fc, :])
            acc = p if acc is None else acc + p
            yield
        out = x + FFN_RES * acc
        if final_norm:
            out = _rms(out, fg_ref[...])
        o_ref[sl, :] = out

    chains = [itertools.chain([None] * part, chain(part)) for part in range(parts)]
    _run_interleaved(chains)


def _ffn(x, g, w_in, w_out, lead, fg, *, tm, parts, final_norm):
    n, d = x.shape
    d_ff = w_out.shape[-2]
    fc = V7X_MXU_DIM
    assert d_ff % fc == 0
    kern = functools.partial(_ffn_kernel, d_ff=d_ff, fc=fc, parts=parts, final_norm=final_norm)
    return pl.pallas_call(
        kern,
        grid=(n // tm,),
        in_specs=[
            pl.BlockSpec((tm, d), lambda i: (i, 0)),
            _const_spec((1, d)),
            _pick_spec(w_in.shape, lead),
            _pick_spec(w_out.shape, lead),
            _const_spec((1, d)),
        ],
        out_specs=pl.BlockSpec((tm, d), lambda i: (i, 0)),
        out_shape=jax.ShapeDtypeStruct((n, d), F32),
        compiler_params=pltpu.CompilerParams(
            dimension_semantics=("arbitrary",), vmem_limit_bytes=VMEM_LIMIT_BYTES),
        name="ffn",
    )(x, g.reshape(1, d), w_in, w_out, fg.reshape(1, d))


def _hgrn_kernel(x_ref, s0_ref, g_ref, win_ref, alb_ref, on_ref, wout_ref,
                 o_ref, s_ref,
                 st_ref, qm_scr, km_scr, qd_scr, ke_scr, qn_scr, kn_scr, v_scr, dec_scr, on_scr, tri_scr,
                 *, bb, gs, tt, chunk, pair, layer_j, dk, dv, shared_init):
    t = pl.program_id(1)
    nt = pl.num_programs(1)
    gn = gs * tt
    hdim = HG_HEADS * dk
    vdim = HG_HEADS * dv
    half = chunk // 2
    span = chunk * pair
    sps = tt // span
    sup = LANES if tt % LANES == 0 else tt
    assert pair in (1, 2) and sup % span == 0

    @pl.when(t == 0)
    def _():
        for s in range(bb):
            for h in range(HG_HEADS):
                st_ref[s, h] = s0_ref[0 if shared_init else s, h].T
        row = lax.broadcasted_iota(jnp.int32, (gn, gn), 0)
        col = lax.broadcasted_iota(jnp.int32, (gn, gn), 1)
        tri_scr[...] = jnp.where((row // chunk == col // chunk) & (col <= row), 1.0, 0.0).astype(BF16)

    alb = alb_ref[...]
    e = jnp.exp(alb - jnp.max(alb, axis=0, keepdims=True))
    lb = jnp.sum(e[:layer_j + 1], axis=0, keepdims=True) / jnp.sum(e, axis=0, keepdims=True)

    srow = lax.broadcasted_iota(jnp.int32, (sup, sup), 0)
    scol = lax.broadcasted_iota(jnp.int32, (sup, sup), 1)
    diag_causal = (srow // chunk == scol // chunk) & (scol <= srow)
    below_diag = (srow // chunk == scol // chunk + 1) & (srow // span == scol // span)

    def recurrence(s):
        base = s * tt
        heads = range(HG_HEADS)
        kls = [slice(h * dk, (h + 1) * dk) for h in heads]
        vls = [slice(h * dv, (h + 1) * dv) for h in heads]
        sup_rows = [pl.ds(pl.multiple_of(base + p * sup, chunk), sup) for p in range(tt // sup)]
        span_rows = [pl.ds(pl.multiple_of(base + c * span, chunk), span) for c in range(sps)]
        sc = [[_dot_nt(qm_scr[r, kls[h]], km_scr[r, kls[h]]) for r in sup_rows] for h in heads]
        if pair == 2:
            nb = [[_dot_nt(qn_scr[r, kls[h]], kn_scr[r, kls[h]]) for r in sup_rows] for h in heads]
            sc = [[jnp.where(below_diag, nb[h][p], sc[h][p]) for p in range(len(sup_rows))] for h in heads]
            wanted = diag_causal | below_diag
        else:
            wanted = diag_causal
        upd = [[_dot_tn(v_scr[r, vls[h]], ke_scr[r, kls[h]]) for r in span_rows] for h in heads]
        intra = [[_dot(jnp.where(wanted, sc[h][p], 0.0).astype(BF16), v_scr[r, vls[h]])
                  for p, r in enumerate(sup_rows)] for h in heads]
        states = []
        for h in heads:
            st = st_ref[s, h]
            at_span_start = []
            for c in range(sps):
                at_span_start.append(st.astype(BF16))
                st = st * dec_scr[s * sps + c, :, kls[h]] + upd[h][c]
            st_ref[s, h] = st
            states.append(at_span_start)
        for h in heads:
            for c, r in enumerate(span_rows):
                p, off = divmod(c * span, sup)
                o = intra[h][p][off:off + span] + _dot_nt(qd_scr[r, kls[h]], states[h][c])
                on_scr[r, vls[h]] = o * lax.rsqrt(jnp.mean(o * o, axis=-1, keepdims=True) + EPS)

    def chain(grp):
        r0 = grp * gn
        x = x_ref[grp * gs:(grp + 1) * gs].reshape(gn, x_ref.shape[-1])
        hn = _rms(x, g_ref[...]).astype(BF16)
        proj = _dot(hn, win_ref[...])
        fr = proj[:, hdim:2 * hdim]
        gate = proj[:, 2 * hdim + vdim:]
        v_scr[r0:r0 + gn, :] = proj[:, 2 * hdim:2 * hdim + vdim].astype(BF16)
        yield

        f = lb + (1.0 - lb) * jax.nn.sigmoid(fr)
        logf = jnp.log(f)
        qs = proj[:, :hdim]
        q = qs * jax.nn.sigmoid(qs)
        k = 1.0 - f
        hi = logf.astype(BF16)
        lo = (logf - hi.astype(F32)).astype(BF16)
        tri = tri_scr[...]
        b = _dot(tri, hi) + _dot(tri, lo)
        for c2 in range(gn // span):
            parts = []
            for c in range(c2 * pair, (c2 + 1) * pair):
                sl = slice(c * chunk, (c + 1) * chunk)
                dst = slice(r0 + c * chunk, r0 + (c + 1) * chunk)
                bc = b[sl]
                mid = bc[half - 1:half, :]
                last = bc[chunk - 1:chunk, :]
                q_mid = q[sl] * jnp.exp(bc - mid)
                k_mid = k[sl] * jnp.exp(mid - bc)
                qm_scr[dst, :] = q_mid.astype(BF16)
                km_scr[dst, :] = k_mid.astype(BF16)
                parts.append((dst, q_mid * jnp.exp(mid), k_mid * jnp.exp(last - mid), jnp.exp(last)))
            if pair == 1:
                (dst, q_in, k_out, dec), = parts
                qd_scr[dst, :] = q_in.astype(BF16)
                ke_scr[dst, :] = k_out.astype(BF16)
            else:
                (dst_a, q_in_a, k_out_a, dec_a), (dst_b, q_in_b, k_out_b, dec_b) = parts
                qn_scr[dst_a, :] = q_in_a.astype(BF16)
                qn_scr[dst_b, :] = q_in_b.astype(BF16)
                kn_scr[dst_a, :] = k_out_a.astype(BF16)
                kn_scr[dst_b, :] = k_out_b.astype(BF16)
                qd_scr[dst_a, :] = q_in_a.astype(BF16)
                qd_scr[dst_b, :] = (q_in_b * dec_a).astype(BF16)
                ke_scr[dst_a, :] = (k_out_a * dec_b).astype(BF16)
                ke_scr[dst_b, :] = k_out_b.astype(BF16)
                dec = dec_a * dec_b
            dec_scr[r0 // span + c2] = dec
        yield

        _for_each(gs, lambda i: recurrence(grp * gs + i))
        yield

        y = on_scr[r0:r0 + gn, :] * on_ref[...] * (gate * jax.nn.sigmoid(gate))
        o_ref[grp * gs:(grp + 1) * gs] = (x + _dot(y.astype(BF16), wout_ref[...])).reshape(gs, tt, x.shape[-1])

    _run_interleaved([chain(grp) for grp in range(bb // gs)])

    @pl.when(t == nt - 1)
    def _():
        for s in range(bb):
            for h in range(HG_HEADS):
                s_ref[s, h] = st_ref[s, h].T


def _hgrn(x, s0, shared_init, g, w_in, a_lb, onorm, w_out, lead, *, bb, gs, tt, chunk, pair, layer_j):
    bsz, tlen, d = x.shape
    _, heads, dk, dv = s0.shape
    n = bb * tt
    shared = shared_init is not None
    kern = functools.partial(_hgrn_kernel, bb=bb, gs=gs, tt=tt, chunk=chunk, pair=pair, layer_j=layer_j, dk=dk, dv=dv,
                             shared_init=shared)
    if shared:
        s0_spec = pl.BlockSpec((1, heads, dk, dv), lambda b, t: (shared_init, 0, 0, 0))
    else:
        s0_spec = pl.BlockSpec((bb, heads, dk, dv), lambda b, t: (b, 0, 0, 0))
    return pl.pallas_call(
        kern,
        grid=(bsz // bb, tlen // tt),
        in_specs=[
            pl.BlockSpec((bb, tt, d), lambda b, t: (b, t, 0)),
            s0_spec,
            _const_spec((1, d)),
            _pick_spec(w_in.shape, lead),
            _const_spec(a_lb.shape),
            _const_spec((1, heads * dv)),
            _pick_spec(w_out.shape, lead),
        ],
        out_specs=[
            pl.BlockSpec((bb, tt, d), lambda b, t: (b, t, 0)),
            pl.BlockSpec((bb, heads, dk, dv), lambda b, t: (b, 0, 0, 0)),
        ],
        out_shape=[
            jax.ShapeDtypeStruct((bsz, tlen, d), F32),
            jax.ShapeDtypeStruct((bsz, heads, dk, dv), F32),
        ],
        scratch_shapes=[
            pltpu.VMEM((bb, heads, dv, dk), F32),
            pltpu.VMEM((n, heads * dk), BF16),
            pltpu.VMEM((n, heads * dk), BF16),
            pltpu.VMEM((n, heads * dk), BF16),
            pltpu.VMEM((n, heads * dk), BF16),
            pltpu.VMEM((n, heads * dk) if pair == 2 else (SUBLANES, LANES), BF16),
            pltpu.VMEM((n, heads * dk) if pair == 2 else (SUBLANES, LANES), BF16),
            pltpu.VMEM((n, heads * dv), BF16),
            pltpu.VMEM((n // (chunk * pair), 1, heads * dk), F32),
            pltpu.VMEM((n, heads * dv), F32),
            pltpu.VMEM((gs * tt, gs * tt), BF16),
        ],
        compiler_params=pltpu.CompilerParams(
            dimension_semantics=("arbitrary", "arbitrary"), vmem_limit_bytes=VMEM_LIMIT_BYTES),
        name="hgrn2",
    )(x, s0, g.reshape(1, d), w_in, a_lb, onorm.reshape(1, heads * dv), w_out)


def _segment_perm(n, tt, transpose):
    seg = tt // SUBLANES
    r = lax.broadcasted_iota(jnp.int32, (n, n), 1 if transpose else 0)
    c = lax.broadcasted_iota(jnp.int32, (n, n), 0 if transpose else 1)
    loc = r % tt
    src = (r - loc) + (loc % SUBLANES) * seg + loc // SUBLANES
    return jnp.where(c == src, 1.0, 0.0).astype(BF16)


def _rglru_kernel(x_ref, h0_ref, c0_ref, g_ref, win_ref, cw_ref, cb_ref, wax_ref,
                  ba_ref, bx_ref, lam_ref, wout_ref,
                  o_ref, h_ref, c_ref,
                  xb_scr, hs_scr, to_seg_scr, from_seg_scr,
                  *, bb, gs, tt, d_rnn, shared_init):
    t = pl.program_id(1)
    gn = gs * tt
    bw = d_rnn // RG_BLOCKS
    tail = CONV_W - 1
    seg = tt // SUBLANES

    @pl.when(t == 0)
    def _():
        for s in range(bb):
            h_ref[s] = h0_ref[0 if shared_init else s]
            c_ref[s] = c0_ref[0 if shared_init else s]
        to_seg_scr[...] = _segment_perm(gn, tt, False)
        from_seg_scr[...] = _segment_perm(gn, tt, True)

    cw = cw_ref[...]
    softplus_neg_lam = jax.nn.softplus(-lam_ref[...])
    sub = lax.broadcasted_iota(jnp.int32, (SUBLANES, d_rnn), 0)

    def stream_pieces(s):
        base = s * tt
        xg = [xb_scr[pl.ds(pl.multiple_of(base + g * SUBLANES, SUBLANES), SUBLANES), :] for g in range(seg)]
        prev = c_ref[s]
        h0 = h_ref[s]

        def delayed(g, j):
            gg, wraps = g - j, 0
            while gg < 0:
                gg, wraps = gg + seg, wraps + 1
            v = xg[gg]
            if wraps:
                v = pltpu.roll(v, wraps, 0)
                for sl in range(wraps):
                    i = tail + sl * seg + g - j
                    v = jnp.where(sub == sl, prev[i:i + 1, :], v)
            return v

        conv = []
        for g in range(seg):
            acc = cb_ref[...] + delayed(g, tail) * cw[0:1, :]
            for j in range(1, CONV_W):
                acc = acc + delayed(g, tail - j) * cw[j:j + 1, :]
            conv.append(acc)
        for i in range(tail):
            step = tt - tail + i
            c_ref[s, i:i + 1, :] = xg[step % seg][step // seg:step // seg + 1, :]
        conv = jnp.concatenate(conv, axis=0)
        yield

        cfb = conv.astype(BF16)
        pre = [_dot(cfb[:, i * bw:(i + 1) * bw], wax_ref[i]) for i in range(RG_BLOCKS)]
        r = jax.nn.sigmoid(jnp.concatenate([p[:, :bw] for p in pre], axis=-1) + ba_ref[...])
        ig = jax.nn.sigmoid(jnp.concatenate([p[:, bw:] for p in pre], axis=-1) + bx_ref[...])
        yield
        log_a = -RG_C * r * softplus_neg_lam
        a = jnp.exp(log_a)
        u = jnp.sqrt(-jnp.tanh(log_a) * (a * a + 1.0)) * (ig * conv)
        yield

        hz = [u[0:SUBLANES]]
        az = [a[0:SUBLANES]]
        for g in range(1, seg):
            ag = a[g * SUBLANES:(g + 1) * SUBLANES]
            hz.append(ag * hz[-1] + u[g * SUBLANES:(g + 1) * SUBLANES])
            az.append(ag * az[-1])
        e_end, a_end = hz[-1], az[-1]
        d = 1
        while d < SUBLANES:
            keep = sub >= d
            e_sh = pltpu.roll(e_end, d, 0)
            a_sh = pltpu.roll(a_end, d, 0)
            e_end = jnp.where(keep, a_end * e_sh + e_end, e_end)
            a_end = jnp.where(keep, a_end * a_sh, a_end)
            d *= 2
        h_end = a_end * h0 + e_end
        h_start = jnp.where(sub == 0, h0, pltpu.roll(h_end, 1, 0))
        h_ref[s] = h_end[SUBLANES - 1:SUBLANES, :]
        for g in range(seg):
            rows = pl.ds(pl.multiple_of(base + g * SUBLANES, SUBLANES), SUBLANES)
            hs_scr[rows, :] = hz[g] + az[g] * h_start

    def chain(grp):
        r0 = grp * gn
        x = x_ref[grp * gs:(grp + 1) * gs].reshape(gn, x_ref.shape[-1])
        hn = _rms(x, g_ref[...]).astype(BF16)
        hp = _dot(to_seg_scr[...], hn).astype(BF16)
        yield
        xb_scr[r0:r0 + gn, :] = _dot(hp, win_ref[:, :d_rnn])
        yield
        gb = _dot(hp, win_ref[:, d_rnn:])
        yield

        if gs == 1:
            yield from stream_pieces(grp)
        else:
            _for_each(gs, lambda i: list(stream_pieces(grp * gs + i)))
        yield

        y = (hs_scr[r0:r0 + gn, :] * jax.nn.gelu(gb)).astype(BF16)
        yield
        yn = _dot(from_seg_scr[...], y).astype(BF16)
        o_ref[grp * gs:(grp + 1) * gs] = (x + _dot(yn, wout_ref[...])).reshape(gs, tt, x.shape[-1])

    _run_interleaved([chain(grp) for grp in range(bb // gs)])


def _rglru(x, h0, c0, shared_init, g, w_in, conv_w, conv_b, wax, ba, bx, lam, w_out, lead, *, bb, gs, tt):
    bsz, tlen, d = x.shape
    d_rnn = h0.shape[-1]
    tail = c0.shape[1]
    n = bb * tt
    shared = shared_init is not None
    kern = functools.partial(_rglru_kernel, bb=bb, gs=gs, tt=tt, d_rnn=d_rnn, shared_init=shared)
    row = lambda v: v.reshape(1, d_rnn)
    if shared:
        init_map, init_n = (lambda b, t: (shared_init, 0, 0)), 1
    else:
        init_map, init_n = (lambda b, t: (b, 0, 0)), bb
    return pl.pallas_call(
        kern,
        grid=(bsz // bb, tlen // tt),
        in_specs=[
            pl.BlockSpec((bb, tt, d), lambda b, t: (b, t, 0)),
            pl.BlockSpec((init_n, 1, d_rnn), init_map),
            pl.BlockSpec((init_n, tail, d_rnn), init_map),
            _const_spec((1, d)),
            _pick_spec(w_in.shape, lead),
            _const_spec(conv_w.shape),
            _const_spec((1, d_rnn)),
            _pick_spec(wax.shape, lead),
            _const_spec((1, d_rnn)),
            _const_spec((1, d_rnn)),
            _const_spec((1, d_rnn)),
            _pick_spec(w_out.shape, lead),
        ],
        out_specs=[
            pl.BlockSpec((bb, tt, d), lambda b, t: (b, t, 0)),
            pl.BlockSpec((bb, 1, d_rnn), lambda b, t: (b, 0, 0)),
            pl.BlockSpec((bb, tail, d_rnn), lambda b, t: (b, 0, 0)),
        ],
        out_shape=[
            jax.ShapeDtypeStruct((bsz, tlen, d), F32),
            jax.ShapeDtypeStruct((bsz, 1, d_rnn), F32),
            jax.ShapeDtypeStruct((bsz, tail, d_rnn), F32),
        ],
        scratch_shapes=[
            pltpu.VMEM((n, d_rnn), F32),
            pltpu.VMEM((n, d_rnn), F32),
            pltpu.VMEM((gs * tt, gs * tt), BF16),
            pltpu.VMEM((gs * tt, gs * tt), BF16),
        ],
        compiler_params=pltpu.CompilerParams(
            dimension_semantics=("arbitrary", "arbitrary"), vmem_limit_bytes=VMEM_LIMIT_BYTES),
        name="rglru",
    )(x, h0, c0, g.reshape(1, d), w_in, conv_w, row(conv_b), wax, row(ba), row(bx), row(lam), w_out)


def _trunk(x, s_hgrn, s_h, s_conv, shared_init, w, *, hgrn_tiles, rglru_tiles, ffn_tiles):
    bsz, tlen, d = x.shape
    depth = w["ffn_norm"].shape[0]
    new_s, new_h, new_c = [], [], []
    h = x.reshape(bsz * tlen, d)
    for layer in range(depth):
        j = layer // N_MIXERS
        h = _ffn(h, w["ffn_norm"][layer, 0], w["ffn_w_in"], w["ffn_w_out"], (layer, 0),
                 w["final_norm"], final_norm=False, **ffn_tiles)
        h3 = h.reshape(bsz, tlen, d)
        if layer % N_MIXERS == 0:
            h3, s_new = _hgrn(h3, s_hgrn[j], shared_init, w["mix_norm"][layer], w["a_w_in"], w["a_lb"],
                              w["a_onorm"][j], w["a_w_out"], (j,), layer_j=j, **hgrn_tiles)
            new_s.append(s_new)
        else:
            h3, h_new, c_new = _rglru(h3, s_h[j], s_conv[j], shared_init, w["mix_norm"][layer], w["b_w_in"],
                                      w["b_conv_w"][j], w["b_conv_b"][j], w["b_wax"], w["b_ba"][j],
                                      w["b_bx"][j], w["b_lambda"][j], w["b_w_out"], (j,), **rglru_tiles)
            new_h.append(h_new)
            new_c.append(c_new)
        h = h3.reshape(bsz * tlen, d)
        h = _ffn(h, w["ffn_norm"][layer, 1], w["ffn_w_in"], w["ffn_w_out"], (layer, 1),
                 w["final_norm"], final_norm=(layer == depth - 1), **ffn_tiles)
    return h.reshape(bsz, tlen, d), new_s, new_h, new_c


def kernel(x_prompt, x_sample, state_hgrn, state_rglru, state_conv, meta_tokens, ffn_norm, ffn_w_in, ffn_w_out, mix_norm, a_w_in, a_lb, a_onorm, a_w_out, b_w_in, b_conv_w, b_conv_b, b_wa, b_ba, b_wx, b_bx, b_lambda, b_w_out, final_norm):
    n_dec, dec_seq, d = x_sample.shape
    n_meta = meta_tokens.shape[0]
    assert n_meta == dec_seq, "the meta prefix is run as one more stream of the short pass"

    w = dict(
        ffn_norm=ffn_norm, mix_norm=mix_norm, final_norm=final_norm,
        ffn_w_in=ffn_w_in.astype(BF16), ffn_w_out=ffn_w_out.astype(BF16),
        a_w_in=a_w_in.astype(BF16), a_lb=a_lb, a_onorm=a_onorm, a_w_out=a_w_out.astype(BF16),
        b_w_in=b_w_in.astype(BF16), b_conv_w=b_conv_w, b_conv_b=b_conv_b,
        b_wax=jnp.concatenate([b_wa, b_wx], axis=-1).astype(BF16),
        b_ba=b_ba, b_bx=b_bx, b_lambda=b_lambda, b_w_out=b_w_out.astype(BF16),
    )

    def with_zero_stream(s):
        return jnp.concatenate([s, jnp.zeros_like(s[:, :1])], axis=1)

    xs = jnp.concatenate([x_sample, meta_tokens.astype(x_sample.dtype)[None]], axis=0)
    n_short = n_dec + 1
    ys, s_s, h_s, c_s = _trunk(
        xs, with_zero_stream(state_hgrn), with_zero_stream(state_rglru)[:, :, None, :],
        with_zero_stream(state_conv), None, w,
        hgrn_tiles=dict(bb=n_short, gs=n_short, tt=dec_seq, chunk=dec_seq, pair=1),
        rglru_tiles=dict(bb=n_short, gs=n_short, tt=dec_seq),
        ffn_tiles=dict(tm=n_short * dec_seq, parts=1))

    yp, s_p, h_p, c_p = _trunk(
        x_prompt, s_s, h_s, c_s, n_dec, w,
        hgrn_tiles=dict(bb=2, gs=1, tt=256, chunk=32, pair=2),
        rglru_tiles=dict(bb=4, gs=1, tt=256),
        ffn_tiles=dict(tm=1024, parts=2))

    return (yp, ys[:n_dec],
            jnp.stack(s_p), jnp.stack([s[:n_dec] for s in s_s]),
            jnp.stack([h[:, 0] for h in h_p]), jnp.stack([h[:n_dec, 0] for h in h_s]),
            jnp.stack(c_p), jnp.stack([c[:n_dec] for c in c_s]))
```

```python
import functools
import itertools

import jax
import jax.numpy as jnp
from jax import lax
from jax.experimental import pallas as pl
from jax.experimental.pallas import tpu as pltpu

F32 = jnp.float32
BF16 = jnp.bfloat16

EPS = 1e-6
FFN_RES = 0.5
RG_C = 8.0
N_MIXERS = 2
HG_HEADS = 8
RG_BLOCKS = 8
CONV_W = 4

V7X_VMEM_BYTES = 64 * 1024 * 1024
VMEM_LIMIT_BYTES = V7X_VMEM_BYTES - 8 * 1024 * 1024
SUBLANES = 8
LANES = 128
V7X_MXU_DIM = 256


def _rms(x, g):
    return x * lax.rsqrt(jnp.mean(x * x, axis=-1, keepdims=True) + EPS) * g


def _dot(a, b):
    return jnp.dot(a, b, preferred_element_type=F32)


def _dot_nt(a, b):
    return lax.dot_general(a, b, (((1,), (1,)), ((), ())), preferred_element_type=F32)


def _dot_tn(a, b):
    return lax.dot_general(a, b, (((0,), (0,)), ((), ())), preferred_element_type=F32)


def _const_spec(shape):
    zeros = (0,) * len(shape)
    return pl.BlockSpec(shape, lambda *_: zeros)


def _pick_spec(shape, lead):
    tail = shape[len(lead):]
    index = tuple(lead) + (0,) * len(tail)
    return pl.BlockSpec((None,) * len(lead) + tuple(tail), lambda *_: index)


def _for_each(count, body):
    if count == 1:
        body(0)
    else:
        lax.fori_loop(0, count, lambda i, c: (body(i), c)[1], 0)


def _run_interleaved(chains):
    for _ in itertools.zip_longest(*chains):
        pass


def _swiglu_partial(yb, wa, wb, wo):
    a = _dot(yb, wa)
    b = _dot(yb, wb)
    return _dot((a * jax.nn.sigmoid(a) * b).astype(BF16), wo)


def _ffn_kernel(x_ref, g_ref, wa_ref, wb_ref, wo_ref, fg_ref, o_ref, *, d_ff, fc, parts, final_norm):
    rows = x_ref.shape[0] // parts

    def chain(part):
        sl = slice(part * rows, (part + 1) * rows)
        x = x_ref[sl, :]
        yb = _rms(x, g_ref[...]).astype(BF16)
        yield
        acc = None
        for lo in range(0, d_ff, fc):
            p = _swiglu_partial(yb, wa_ref[:, lo:lo + fc], wb_ref[:, lo:lo + fc], wo_ref[lo:lo + fc, :])
            acc = p if acc is None else acc + p
            yield
        out = x + FFN_RES * acc
        if final_norm:
            out = _rms(out, fg_ref[...])
        o_ref[sl, :] = out

    chains = [itertools.chain([None] * part, chain(part)) for part in range(parts)]
    _run_interleaved(chains)


def _ffn(x, g, wa, wb, wo, fg, *, tm, parts, final_norm):
    n, d = x.shape
    d_ff = wo.shape[0]
    fc = V7X_MXU_DIM
    assert d_ff % fc == 0
    kern = functools.partial(_ffn_kernel, d_ff=d_ff, fc=fc, parts=parts, final_norm=final_norm)
    return pl.pallas_call(
        kern,
        grid=(n // tm,),
        in_specs=[
            pl.BlockSpec((tm, d), lambda i: (i, 0)),
            _const_spec((1, d)),
            _const_spec(wa.shape),
            _const_spec(wb.shape),
            _const_spec(wo.shape),
            _const_spec((1, d)),
        ],
        out_specs=pl.BlockSpec((tm, d), lambda i: (i, 0)),
        out_shape=jax.ShapeDtypeStruct((n, d), F32),
        compiler_params=pltpu.CompilerParams(
            dimension_semantics=("arbitrary",), vmem_limit_bytes=VMEM_LIMIT_BYTES),
        name="ffn",
    )(x, g.reshape(1, d), wa, wb, wo, fg.reshape(1, d))


def _ffn_cast_kernel(x_ref, g_ref, wa_ref, wb_ref, wo_ref, fg_ref,
                     o_ref, wa16_ref, wb16_ref, wo16_ref,
                     yb_scr, acc_scr, *, final_norm):
    c = pl.program_id(0)

    @pl.when(c == 0)
    def _():
        yb_scr[...] = _rms(x_ref[...], g_ref[...]).astype(BF16)
        acc_scr[...] = jnp.zeros_like(acc_scr)

    wa = wa_ref[...].astype(BF16)
    wb = wb_ref[...].astype(BF16)
    wo = wo_ref[...].astype(BF16)
    wa16_ref[...] = wa
    wb16_ref[...] = wb
    wo16_ref[...] = wo
    acc_scr[...] += _swiglu_partial(yb_scr[...], wa, wb, wo)

    @pl.when(c == pl.num_programs(0) - 1)
    def _():
        out = x_ref[...] + FFN_RES * acc_scr[...]
        if final_norm:
            out = _rms(out, fg_ref[...])
        o_ref[...] = out


def _ffn_cast(x, g, w_in, w_out, lead, fg, *, final_norm):
    n, d = x.shape
    d_ff = w_out.shape[-2]
    fc = V7X_MXU_DIM
    steps = d_ff // fc
    assert d_ff % fc == 0
    none = (None,) * len(lead)
    kern = functools.partial(_ffn_cast_kernel, final_norm=final_norm)
    return pl.pallas_call(
        kern,
        grid=(steps,),
        in_specs=[
            _const_spec((n, d)),
            _const_spec((1, d)),
            pl.BlockSpec(none + (d, fc), lambda c: tuple(lead) + (0, c)),
            pl.BlockSpec(none + (d, fc), lambda c: tuple(lead) + (0, steps + c)),
            pl.BlockSpec(none + (fc, d), lambda c: tuple(lead) + (c, 0)),
            _const_spec((1, d)),
        ],
        out_specs=[
            _const_spec((n, d)),
            pl.BlockSpec((d, fc), lambda c: (0, c)),
            pl.BlockSpec((d, fc), lambda c: (0, c)),
            pl.BlockSpec((fc, d), lambda c: (c, 0)),
        ],
        out_shape=[
            jax.ShapeDtypeStruct((n, d), F32),
            jax.ShapeDtypeStruct((d, d_ff), BF16),
            jax.ShapeDtypeStruct((d, d_ff), BF16),
            jax.ShapeDtypeStruct((d_ff, d), BF16),
        ],
        scratch_shapes=[pltpu.VMEM((n, d), BF16), pltpu.VMEM((n, d), F32)],
        compiler_params=pltpu.CompilerParams(
            dimension_semantics=("arbitrary",), vmem_limit_bytes=VMEM_LIMIT_BYTES),
        name="ffn_cast",
    )(x, g.reshape(1, d), w_in, w_in, w_out, fg.reshape(1, d))


def _hgrn_kernel(x_ref, s0_ref, g_ref, win_ref, alb_ref, on_ref, wout_ref,
                 o_ref, s_ref,
                 st_ref, qm_scr, km_scr, qd_scr, ke_scr, qn_scr, kn_scr, v_scr, dec_scr, on_scr, tri_scr,
                 *, bb, gs, tt, chunk, pair, layer_j, dk, dv, shared_init):
    t = pl.program_id(1)
    nt = pl.num_programs(1)
    gn = gs * tt
    hdim = HG_HEADS * dk
    vdim = HG_HEADS * dv
    half = chunk // 2
    span = chunk * pair
    sps = tt // span
    sup = LANES if tt % LANES == 0 else tt
    assert pair in (1, 2) and sup % span == 0

    @pl.when(t == 0)
    def _():
        for s in range(bb):
            for h in range(HG_HEADS):
                st_ref[s, h] = s0_ref[0 if shared_init else s, h].T
        row = lax.broadcasted_iota(jnp.int32, (gn, gn), 0)
        col = lax.broadcasted_iota(jnp.int32, (gn, gn), 1)
        tri_scr[...] = jnp.where((row // chunk == col // chunk) & (col <= row), 1.0, 0.0).astype(BF16)

    alb = alb_ref[...]
    e = jnp.exp(alb - jnp.max(alb, axis=0, keepdims=True))
    lb = jnp.sum(e[:layer_j + 1], axis=0, keepdims=True) / jnp.sum(e, axis=0, keepdims=True)

    srow = lax.broadcasted_iota(jnp.int32, (sup, sup), 0)
    scol = lax.broadcasted_iota(jnp.int32, (sup, sup), 1)
    diag_causal = (srow // chunk == scol // chunk) & (scol <= srow)
    below_diag = (srow // chunk == scol // chunk + 1) & (srow // span == scol // span)

    def recurrence(s):
        base = s * tt
        heads = range(HG_HEADS)
        kls = [slice(h * dk, (h + 1) * dk) for h in heads]
        vls = [slice(h * dv, (h + 1) * dv) for h in heads]
        sup_rows = [pl.ds(pl.multiple_of(base + p * sup, chunk), sup) for p in range(tt // sup)]
        span_rows = [pl.ds(pl.multiple_of(base + c * span, chunk), span) for c in range(sps)]
        sc = [[_dot_nt(qm_scr[r, kls[h]], km_scr[r, kls[h]]) for r in sup_rows] for h in heads]
        if pair == 2:
            nb = [[_dot_nt(qn_scr[r, kls[h]], kn_scr[r, kls[h]]) for r in sup_rows] for h in heads]
            sc = [[jnp.where(below_diag, nb[h][p], sc[h][p]) for p in range(len(sup_rows))] for h in heads]
            wanted = diag_causal | below_diag
        else:
            wanted = diag_causal
        upd = [[_dot_tn(v_scr[r, vls[h]], ke_scr[r, kls[h]]) for r in span_rows] for h in heads]
        intra = [[_dot(jnp.where(wanted, sc[h][p], 0.0).astype(BF16), v_scr[r, vls[h]])
                  for p, r in enumerate(sup_rows)] for h in heads]
        states = []
        for h in heads:
            st = st_ref[s, h]
            at_span_start = []
            for c in range(sps):
                at_span_start.append(st.astype(BF16))
                st = st * dec_scr[s * sps + c, :, kls[h]] + upd[h][c]
            st_ref[s, h] = st
            states.append(at_span_start)
        for h in heads:
            for c, r in enumerate(span_rows):
                p, off = divmod(c * span, sup)
                o = intra[h][p][off:off + span] + _dot_nt(qd_scr[r, kls[h]], states[h][c])
                on_scr[r, vls[h]] = o * lax.rsqrt(jnp.mean(o * o, axis=-1, keepdims=True) + EPS)

    def chain(grp):
        r0 = grp * gn
        x = x_ref[grp * gs:(grp + 1) * gs].reshape(gn, x_ref.shape[-1])
        hn = _rms(x, g_ref[...]).astype(BF16)
        proj = _dot(hn, win_ref[...])
        fr = proj[:, hdim:2 * hdim]
        gate = proj[:, 2 * hdim + vdim:]
        v_scr[r0:r0 + gn, :] = proj[:, 2 * hdim:2 * hdim + vdim].astype(BF16)
        yield

        f = lb + (1.0 - lb) * jax.nn.sigmoid(fr)
        logf = jnp.log(f)
        qs = proj[:, :hdim]
        q = qs * jax.nn.sigmoid(qs)
        k = 1.0 - f
        hi = logf.astype(BF16)
        lo = (logf - hi.astype(F32)).astype(BF16)
        tri = tri_scr[...]
        b = _dot(tri, hi) + _dot(tri, lo)
        for c2 in range(gn // span):
            parts = []
            for c in range(c2 * pair, (c2 + 1) * pair):
                sl = slice(c * chunk, (c + 1) * chunk)
                dst = slice(r0 + c * chunk, r0 + (c + 1) * chunk)
                bc = b[sl]
                mid = bc[half - 1:half, :]
                last = bc[chunk - 1:chunk, :]
                q_mid = q[sl] * jnp.exp(bc - mid)
                k_mid = k[sl] * jnp.exp(mid - bc)
                qm_scr[dst, :] = q_mid.astype(BF16)
                km_scr[dst, :] = k_mid.astype(BF16)
                parts.append((dst, q_mid * jnp.exp(mid), k_mid * jnp.exp(last - mid), jnp.exp(last)))
            if pair == 1:
                (dst, q_in, k_out, dec), = parts
                qd_scr[dst, :] = q_in.astype(BF16)
                ke_scr[dst, :] = k_out.astype(BF16)
            else:
                (dst_a, q_in_a, k_out_a, dec_a), (dst_b, q_in_b, k_out_b, dec_b) = parts
                qn_scr[dst_a, :] = q_in_a.astype(BF16)
                qn_scr[dst_b, :] = q_in_b.astype(BF16)
                kn_scr[dst_a, :] = k_out_a.astype(BF16)
                kn_scr[dst_b, :] = k_out_b.astype(BF16)
                qd_scr[dst_a, :] = q_in_a.astype(BF16)
                qd_scr[dst_b, :] = (q_in_b * dec_a).astype(BF16)
                ke_scr[dst_a, :] = (k_out_a * dec_b).astype(BF16)
                ke_scr[dst_b, :] = k_out_b.astype(BF16)
                dec = dec_a * dec_b
            dec_scr[r0 // span + c2] = dec
        yield

        _for_each(gs, lambda i: recurrence(grp * gs + i))
        yield

        y = on_scr[r0:r0 + gn, :] * on_ref[...] * (gate * jax.nn.sigmoid(gate))
        o_ref[grp * gs:(grp + 1) * gs] = (x + _dot(y.astype(BF16), wout_ref[...])).reshape(gs, tt, x.shape[-1])

    _run_interleaved([chain(grp) for grp in range(bb // gs)])

    @pl.when(t == nt - 1)
    def _():
        for s in range(bb):
            for h in range(HG_HEADS):
                s_ref[s, h] = st_ref[s, h].T


def _hgrn(x, s0, shared_init, g, w_in, a_lb, onorm, w_out, lead, *, bb, gs, tt, chunk, pair, layer_j):
    bsz, tlen, d = x.shape
    _, heads, dk, dv = s0.shape
    n = bb * tt
    shared = shared_init is not None
    kern = functools.partial(_hgrn_kernel, bb=bb, gs=gs, tt=tt, chunk=chunk, pair=pair, layer_j=layer_j, dk=dk, dv=dv,
                             shared_init=shared)
    if shared:
        s0_spec = pl.BlockSpec((1, heads, dk, dv), lambda b, t: (shared_init, 0, 0, 0))
    else:
        s0_spec = pl.BlockSpec((bb, heads, dk, dv), lambda b, t: (b, 0, 0, 0))
    return pl.pallas_call(
        kern,
        grid=(bsz // bb, tlen // tt),
        in_specs=[
            pl.BlockSpec((bb, tt, d), lambda b, t: (b, t, 0)),
            s0_spec,
            _const_spec((1, d)),
            _pick_spec(w_in.shape, lead),
            _const_spec(a_lb.shape),
            _const_spec((1, heads * dv)),
            _pick_spec(w_out.shape, lead),
        ],
        out_specs=[
            pl.BlockSpec((bb, tt, d), lambda b, t: (b, t, 0)),
            pl.BlockSpec((bb, heads, dk, dv), lambda b, t: (b, 0, 0, 0)),
        ],
        out_shape=[
            jax.ShapeDtypeStruct((bsz, tlen, d), F32),
            jax.ShapeDtypeStruct((bsz, heads, dk, dv), F32),
        ],
        scratch_shapes=[
            pltpu.VMEM((bb, heads, dv, dk), F32),
            pltpu.VMEM((n, heads * dk), BF16),
            pltpu.VMEM((n, heads * dk), BF16),
            pltpu.VMEM((n, heads * dk), BF16),
            pltpu.VMEM((n, heads * dk), BF16),
            pltpu.VMEM((n, heads * dk) if pair == 2 else (SUBLANES, LANES), BF16),
            pltpu.VMEM((n, heads * dk) if pair == 2 else (SUBLANES, LANES), BF16),
            pltpu.VMEM((n, heads * dv), BF16),
            pltpu.VMEM((n // (chunk * pair), 1, heads * dk), F32),
            pltpu.VMEM((n, heads * dv), F32),
            pltpu.VMEM((gs * tt, gs * tt), BF16),
        ],
        compiler_params=pltpu.CompilerParams(
            dimension_semantics=("arbitrary", "arbitrary"), vmem_limit_bytes=VMEM_LIMIT_BYTES),
        name="hgrn2",
    )(x, s0, g.reshape(1, d), w_in, a_lb, onorm.reshape(1, heads * dv), w_out)


def _segment_perm(n, tt, transpose):
    seg = tt // SUBLANES
    r = lax.broadcasted_iota(jnp.int32, (n, n), 1 if transpose else 0)
    c = lax.broadcasted_iota(jnp.int32, (n, n), 0 if transpose else 1)
    loc = r % tt
    src = (r - loc) + (loc % SUBLANES) * seg + loc // SUBLANES
    return jnp.where(c == src, 1.0, 0.0).astype(BF16)


def _rglru_kernel(x_ref, h0_ref, c0_ref, g_ref, win_ref, cw_ref, cb_ref, wax_ref,
                  ba_ref, bx_ref, lam_ref, wout_ref,
                  o_ref, h_ref, c_ref,
                  xb_scr, hs_scr, to_seg_scr, from_seg_scr,
                  *, bb, gs, tt, d_rnn, shared_init):
    t = pl.program_id(1)
    gn = gs * tt
    bw = d_rnn // RG_BLOCKS
    tail = CONV_W - 1
    seg = tt // SUBLANES

    @pl.when(t == 0)
    def _():
        for s in range(bb):
            h_ref[s] = h0_ref[0 if shared_init else s]
            c_ref[s] = c0_ref[0 if shared_init else s]
        to_seg_scr[...] = _segment_perm(gn, tt, False)
        from_seg_scr[...] = _segment_perm(gn, tt, True)

    cw = cw_ref[...]
    softplus_neg_lam = jax.nn.softplus(-lam_ref[...])
    sub = lax.broadcasted_iota(jnp.int32, (SUBLANES, d_rnn), 0)

    def stream_pieces(s):
        base = s * tt
        xg = [xb_scr[pl.ds(pl.multiple_of(base + g * SUBLANES, SUBLANES), SUBLANES), :] for g in range(seg)]
        prev = c_ref[s]
        h0 = h_ref[s]

        def delayed(g, j):
            gg, wraps = g - j, 0
            while gg < 0:
                gg, wraps = gg + seg, wraps + 1
            v = xg[gg]
            if wraps:
                v = pltpu.roll(v, wraps, 0)
                for sl in range(wraps):
                    i = tail + sl * seg + g - j
                    v = jnp.where(sub == sl, prev[i:i + 1, :], v)
            return v

        conv = []
        for g in range(seg):
            acc = cb_ref[...] + delayed(g, tail) * cw[0:1, :]
            for j in range(1, CONV_W):
                acc = acc + delayed(g, tail - j) * cw[j:j + 1, :]
            conv.append(acc)
        for i in range(tail):
            step = tt - tail + i
            c_ref[s, i:i + 1, :] = xg[step % seg][step // seg:step // seg + 1, :]
        conv = jnp.concatenate(conv, axis=0)
        yield

        cfb = conv.astype(BF16)
        pre = [_dot(cfb[:, i * bw:(i + 1) * bw], wax_ref[i]) for i in range(RG_BLOCKS)]
        r = jax.nn.sigmoid(jnp.concatenate([p[:, :bw] for p in pre], axis=-1) + ba_ref[...])
        ig = jax.nn.sigmoid(jnp.concatenate([p[:, bw:] for p in pre], axis=-1) + bx_ref[...])
        yield
        log_a = -RG_C * r * softplus_neg_lam
        a = jnp.exp(log_a)
        u = jnp.sqrt(-jnp.tanh(log_a) * (a * a + 1.0)) * (ig * conv)
        yield

        hz = [u[0:SUBLANES]]
        az = [a[0:SUBLANES]]
        for g in range(1, seg):
            ag = a[g * SUBLANES:(g + 1) * SUBLANES]
            hz.append(ag * hz[-1] + u[g * SUBLANES:(g + 1) * SUBLANES])
            az.append(ag * az[-1])
        e_end, a_end = hz[-1], az[-1]
        d = 1
        while d < SUBLANES:
            keep = sub >= d
            e_sh = pltpu.roll(e_end, d, 0)
            a_sh = pltpu.roll(a_end, d, 0)
            e_end = jnp.where(keep, a_end * e_sh + e_end, e_end)
            a_end = jnp.where(keep, a_end * a_sh, a_end)
            d *= 2
        h_end = a_end * h0 + e_end
        h_start = jnp.where(sub == 0, h0, pltpu.roll(h_end, 1, 0))
        h_ref[s] = h_end[SUBLANES - 1:SUBLANES, :]
        for g in range(seg):
            rows = pl.ds(pl.multiple_of(base + g * SUBLANES, SUBLANES), SUBLANES)
            hs_scr[rows, :] = hz[g] + az[g] * h_start

    def chain(grp):
        r0 = grp * gn
        x = x_ref[grp * gs:(grp + 1) * gs].reshape(gn, x_ref.shape[-1])
        hn = _rms(x, g_ref[...]).astype(BF16)
        hp = _dot(to_seg_scr[...], hn).astype(BF16)
        yield
        xb_scr[r0:r0 + gn, :] = _dot(hp, win_ref[:, :d_rnn])
        yield
        gb = _dot(hp, win_ref[:, d_rnn:])
        yield

        if gs == 1:
            yield from stream_pieces(grp)
        else:
            _for_each(gs, lambda i: list(stream_pieces(grp * gs + i)))
        yield

        y = (hs_scr[r0:r0 + gn, :] * jax.nn.gelu(gb)).astype(BF16)
        yield
        yn = _dot(from_seg_scr[...], y).astype(BF16)
        o_ref[grp * gs:(grp + 1) * gs] = (x + _dot(yn, wout_ref[...])).reshape(gs, tt, x.shape[-1])

    _run_interleaved([chain(grp) for grp in range(bb // gs)])


def _rglru(x, h0, c0, shared_init, g, w_in, conv_w, conv_b, wax, ba, bx, lam, w_out, lead, *, bb, gs, tt):
    bsz, tlen, d = x.shape
    d_rnn = h0.shape[-1]
    tail = c0.shape[1]
    n = bb * tt
    shared = shared_init is not None
    kern = functools.partial(_rglru_kernel, bb=bb, gs=gs, tt=tt, d_rnn=d_rnn, shared_init=shared)
    row = lambda v: v.reshape(1, d_rnn)
    if shared:
        init_map, init_n = (lambda b, t: (shared_init, 0, 0)), 1
    else:
        init_map, init_n = (lambda b, t: (b, 0, 0)), bb
    return pl.pallas_call(
        kern,
        grid=(bsz // bb, tlen // tt),
        in_specs=[
            pl.BlockSpec((bb, tt, d), lambda b, t: (b, t, 0)),
            pl.BlockSpec((init_n, 1, d_rnn), init_map),
            pl.BlockSpec((init_n, tail, d_rnn), init_map),
            _const_spec((1, d)),
            _pick_spec(w_in.shape, lead),
            _const_spec(conv_w.shape),
            _const_spec((1, d_rnn)),
            _pick_spec(wax.shape, lead),
            _const_spec((1, d_rnn)),
            _const_spec((1, d_rnn)),
            _const_spec((1, d_rnn)),
            _pick_spec(w_out.shape, lead),
        ],
        out_specs=[
            pl.BlockSpec((bb, tt, d), lambda b, t: (b, t, 0)),
            pl.BlockSpec((bb, 1, d_rnn), lambda b, t: (b, 0, 0)),
            pl.BlockSpec((bb, tail, d_rnn), lambda b, t: (b, 0, 0)),
        ],
        out_shape=[
            jax.ShapeDtypeStruct((bsz, tlen, d), F32),
            jax.ShapeDtypeStruct((bsz, 1, d_rnn), F32),
            jax.ShapeDtypeStruct((bsz, tail, d_rnn), F32),
        ],
        scratch_shapes=[
            pltpu.VMEM((n, d_rnn), F32),
            pltpu.VMEM((n, d_rnn), F32),
            pltpu.VMEM((gs * tt, gs * tt), BF16),
            pltpu.VMEM((gs * tt, gs * tt), BF16),
        ],
        compiler_params=pltpu.CompilerParams(
            dimension_semantics=("arbitrary", "arbitrary"), vmem_limit_bytes=VMEM_LIMIT_BYTES),
        name="rglru",
    )(x, h0, c0, g.reshape(1, d), w_in, conv_w, row(conv_b), wax, row(ba), row(bx), row(lam), w_out)


def _trunk(x, s_hgrn, s_h, s_conv, shared_init, w, ffn, *, hgrn_tiles, rglru_tiles):
    bsz, tlen, d = x.shape
    depth = w["mix_norm"].shape[0]
    new_s, new_h, new_c = [], [], []
    h = x.reshape(bsz * tlen, d)
    for layer in range(depth):
        j = layer // N_MIXERS
        h = ffn(h, layer, 0, False)
        h3 = h.reshape(bsz, tlen, d)
        if layer % N_MIXERS == 0:
            h3, s_new = _hgrn(h3, s_hgrn[j], shared_init, w["mix_norm"][layer], w["a_w_in"], w["a_lb"],
                              w["a_onorm"][j], w["a_w_out"], (j,), layer_j=j, **hgrn_tiles)
            new_s.append(s_new)
        else:
            h3, h_new, c_new = _rglru(h3, s_h[j], s_conv[j], shared_init, w["mix_norm"][layer], w["b_w_in"],
                                      w["b_conv_w"][j], w["b_conv_b"][j], w["b_wax"], w["b_ba"][j],
                                      w["b_bx"][j], w["b_lambda"][j], w["b_w_out"], (j,), **rglru_tiles)
            new_h.append(h_new)
            new_c.append(c_new)
        h = h3.reshape(bsz * tlen, d)
        h = ffn(h, layer, 1, layer == depth - 1)
    return h.reshape(bsz, tlen, d), new_s, new_h, new_c


def kernel(x_prompt, x_sample, state_hgrn, state_rglru, state_conv, meta_tokens, ffn_norm, ffn_w_in, ffn_w_out, mix_norm, a_w_in, a_lb, a_onorm, a_w_out, b_w_in, b_conv_w, b_conv_b, b_wa, b_ba, b_wx, b_bx, b_lambda, b_w_out, final_norm):
    n_dec, dec_seq, d = x_sample.shape
    n_meta = meta_tokens.shape[0]
    assert n_meta == dec_seq, "the meta prefix is run as one more stream of the short pass"

    w = dict(
        mix_norm=mix_norm,
        a_w_in=a_w_in.astype(BF16), a_lb=a_lb, a_onorm=a_onorm, a_w_out=a_w_out.astype(BF16),
        b_w_in=b_w_in.astype(BF16), b_conv_w=b_conv_w, b_conv_b=b_conv_b,
        b_wax=jnp.concatenate([b_wa, b_wx], axis=-1).astype(BF16),
        b_ba=b_ba, b_bx=b_bx, b_lambda=b_lambda, b_w_out=b_w_out.astype(BF16),
    )

    def with_zero_stream(s):
        return jnp.concatenate([s, jnp.zeros_like(s[:, :1])], axis=1)

    xs = jnp.concatenate([x_sample, meta_tokens.astype(x_sample.dtype)[None]], axis=0)
    n_short = n_dec + 1

    ffn_bf16 = {}

    def ffn_short(h, layer, slot, last):
        h, *ffn_bf16[layer, slot] = _ffn_cast(
            h, ffn_norm[layer, slot], ffn_w_in, ffn_w_out, (layer, slot), final_norm,
            final_norm=last)
        return h

    def ffn_long(h, layer, slot, last):
        return _ffn(h, ffn_norm[layer, slot], *ffn_bf16[layer, slot], final_norm,
                    tm=1024, parts=2, final_norm=last)

    ys, s_s, h_s, c_s = _trunk(
        xs, with_zero_stream(state_hgrn), with_zero_stream(state_rglru)[:, :, None, :],
        with_zero_stream(state_conv), None, w, ffn_short,
        hgrn_tiles=dict(bb=n_short, gs=n_short, tt=dec_seq, chunk=dec_seq, pair=1),
        rglru_tiles=dict(bb=n_short, gs=n_short, tt=dec_seq))

    yp, s_p, h_p, c_p = _trunk(
        x_prompt, s_s, h_s, c_s, n_dec, w, ffn_long,
        hgrn_tiles=dict(bb=2, gs=1, tt=256, chunk=32, pair=2),
        rglru_tiles=dict(bb=4, gs=1, tt=256))

    return (yp, ys[:n_dec],
            jnp.stack(s_p), jnp.stack([s[:n_dec] for s in s_s]),
            jnp.stack([h[:, 0] for h in h_p]), jnp.stack([h[:n_dec, 0] for h in h_s]),
            jnp.stack(c_p), jnp.stack([c[:n_dec] for c in c_s]))
```

```python
import functools
import itertools

import jax
import jax.numpy as jnp
from jax import lax
from jax.experimental import pallas as pl
from jax.experimental.pallas import tpu as pltpu

F32 = jnp.float32
BF16 = jnp.bfloat16

EPS = 1e-6
FFN_RES = 0.5
RG_C = 8.0
N_MIXERS = 2
HG_HEADS = 8
RG_BLOCKS = 8
CONV_W = 4

V7X_VMEM_BYTES = 64 * 1024 * 1024
VMEM_LIMIT_BYTES = V7X_VMEM_BYTES - 8 * 1024 * 1024
SUBLANES = 8
LANES = 128
V7X_MXU_DIM = 256


def _rms(x, g):
    return x * lax.rsqrt(jnp.mean(x * x, axis=-1, keepdims=True) + EPS) * g


def _sigmoid(x):
    return 0.5 * jnp.tanh(0.5 * x) + 0.5


def _silu(x):
    h = 0.5 * x
    return h * jnp.tanh(h) + h


def _dot(a, b):
    return jnp.dot(a, b, preferred_element_type=F32)


def _dot_nt(a, b):
    return lax.dot_general(a, b, (((1,), (1,)), ((), ())), preferred_element_type=F32)


def _dot_tn(a, b):
    return lax.dot_general(a, b, (((0,), (0,)), ((), ())), preferred_element_type=F32)


def _const_spec(shape):
    zeros = (0,) * len(shape)
    return pl.BlockSpec(shape, lambda *_: zeros)


def _pick_spec(shape, lead):
    tail = shape[len(lead):]
    index = tuple(lead) + (0,) * len(tail)
    return pl.BlockSpec((None,) * len(lead) + tuple(tail), lambda *_: index)


def _for_each(count, body):
    if count == 1:
        body(0)
    else:
        lax.fori_loop(0, count, lambda i, c: (body(i), c)[1], 0)


def _run_interleaved(chains):
    for _ in itertools.zip_longest(*chains):
        pass


def _swiglu_partial(yb, wa, wb, wo):
    a = _dot(yb, wa)
    b = _dot(yb, wb)
    return _dot((_silu(a) * b).astype(BF16), wo)


def _ffn_kernel(x_ref, g_ref, wa_ref, wb_ref, wo_ref, fg_ref, o_ref, *, d_ff, fc, parts, final_norm):
    rows = x_ref.shape[0] // parts

    def chain(part):
        sl = slice(part * rows, (part + 1) * rows)
        x = x_ref[sl, :]
        yb = _rms(x, g_ref[...]).astype(BF16)
        yield
        acc = None
        for lo in range(0, d_ff, fc):
            p = _swiglu_partial(yb, wa_ref[:, lo:lo + fc], wb_ref[:, lo:lo + fc], wo_ref[lo:lo + fc, :])
            acc = p if acc is None else acc + p
            yield
        out = x + FFN_RES * acc
        if final_norm:
            out = _rms(out, fg_ref[...])
        o_ref[sl, :] = out

    chains = [itertools.chain([None] * part, chain(part)) for part in range(parts)]
    _run_interleaved(chains)


def _ffn(x, g, wa, wb, wo, fg, *, tm, parts, final_norm):
    n, d = x.shape
    d_ff = wo.shape[0]
    fc = V7X_MXU_DIM
    assert d_ff % fc == 0
    kern = functools.partial(_ffn_kernel, d_ff=d_ff, fc=fc, parts=parts, final_norm=final_norm)
    return pl.pallas_call(
        kern,
        grid=(n // tm,),
        in_specs=[
            pl.BlockSpec((tm, d), lambda i: (i, 0)),
            _const_spec((1, d)),
            _const_spec(wa.shape),
            _const_spec(wb.shape),
            _const_spec(wo.shape),
            _const_spec((1, d)),
        ],
        out_specs=pl.BlockSpec((tm, d), lambda i: (i, 0)),
        out_shape=jax.ShapeDtypeStruct((n, d), F32),
        compiler_params=pltpu.CompilerParams(
            dimension_semantics=("arbitrary",), vmem_limit_bytes=VMEM_LIMIT_BYTES),
        name="ffn",
    )(x, g.reshape(1, d), wa, wb, wo, fg.reshape(1, d))


def _ffn_cast_kernel(x_ref, g_ref, wa_ref, wb_ref, wo_ref, fg_ref,
                     o_ref, wa16_ref, wb16_ref, wo16_ref,
                     yb_scr, acc_scr, *, final_norm):
    c = pl.program_id(0)

    @pl.when(c == 0)
    def _():
        yb_scr[...] = _rms(x_ref[...], g_ref[...]).astype(BF16)
        acc_scr[...] = jnp.zeros_like(acc_scr)

    wa = wa_ref[...].astype(BF16)
    wb = wb_ref[...].astype(BF16)
    wo = wo_ref[...].astype(BF16)
    wa16_ref[...] = wa
    wb16_ref[...] = wb
    wo16_ref[...] = wo
    acc_scr[...] += _swiglu_partial(yb_scr[...], wa, wb, wo)

    @pl.when(c == pl.num_programs(0) - 1)
    def _():
        out = x_ref[...] + FFN_RES * acc_scr[...]
        if final_norm:
            out = _rms(out, fg_ref[...])
        o_ref[...] = out


def _ffn_cast(x, g, w_in, w_out, lead, fg, *, final_norm):
    n, d = x.shape
    d_ff = w_out.shape[-2]
    fc = V7X_MXU_DIM
    steps = d_ff // fc
    assert d_ff % fc == 0
    none = (None,) * len(lead)
    kern = functools.partial(_ffn_cast_kernel, final_norm=final_norm)
    return pl.pallas_call(
        kern,
        grid=(steps,),
        in_specs=[
            _const_spec((n, d)),
            _const_spec((1, d)),
            pl.BlockSpec(none + (d, fc), lambda c: tuple(lead) + (0, c)),
            pl.BlockSpec(none + (d, fc), lambda c: tuple(lead) + (0, steps + c)),
            pl.BlockSpec(none + (fc, d), lambda c: tuple(lead) + (c, 0)),
            _const_spec((1, d)),
        ],
        out_specs=[
            _const_spec((n, d)),
            pl.BlockSpec((d, fc), lambda c: (0, c)),
            pl.BlockSpec((d, fc), lambda c: (0, c)),
            pl.BlockSpec((fc, d), lambda c: (c, 0)),
        ],
        out_shape=[
            jax.ShapeDtypeStruct((n, d), F32),
            jax.ShapeDtypeStruct((d, d_ff), BF16),
            jax.ShapeDtypeStruct((d, d_ff), BF16),
            jax.ShapeDtypeStruct((d_ff, d), BF16),
        ],
        scratch_shapes=[pltpu.VMEM((n, d), BF16), pltpu.VMEM((n, d), F32)],
        compiler_params=pltpu.CompilerParams(
            dimension_semantics=("arbitrary",), vmem_limit_bytes=VMEM_LIMIT_BYTES),
        name="ffn_cast",
    )(x, g.reshape(1, d), w_in, w_in, w_out, fg.reshape(1, d))


def _hgrn_kernel(x_ref, s0_ref, g_ref, win_ref, alb_ref, on_ref, wout_ref,
                 o_ref, s_ref,
                 st_ref, qm_scr, km_scr, qd_scr, ke_scr, qn_scr, kn_scr, v_scr, dec_scr, on_scr, tri_scr,
                 *, bb, gs, tt, chunk, pair, layer_j, dk, dv, shared_init):
    t = pl.program_id(1)
    nt = pl.num_programs(1)
    gn = gs * tt
    hdim = HG_HEADS * dk
    vdim = HG_HEADS * dv
    half = chunk // 2
    span = chunk * pair
    sps = tt // span
    sup = LANES if tt % LANES == 0 else tt
    assert pair in (1, 2) and sup % span == 0

    @pl.when(t == 0)
    def _():
        for s in range(bb):
            for h in range(HG_HEADS):
                st_ref[s, h] = s0_ref[0 if shared_init else s, h].T
        row = lax.broadcasted_iota(jnp.int32, (gn, gn), 0)
        col = lax.broadcasted_iota(jnp.int32, (gn, gn), 1)
        tri_scr[...] = jnp.where((row // chunk == col // chunk) & (col <= row), 1.0, 0.0).astype(BF16)

    alb = alb_ref[...]
    e = jnp.exp(alb - jnp.max(alb, axis=0, keepdims=True))
    lb = jnp.sum(e[:layer_j + 1], axis=0, keepdims=True) / jnp.sum(e, axis=0, keepdims=True)

    srow = lax.broadcasted_iota(jnp.int32, (sup, sup), 0)
    scol = lax.broadcasted_iota(jnp.int32, (sup, sup), 1)
    diag_causal = (srow // chunk == scol // chunk) & (scol <= srow)
    below_diag = (srow // chunk == scol // chunk + 1) & (srow // span == scol // span)

    def recurrence(s):
        base = s * tt
        heads = range(HG_HEADS)
        kls = [slice(h * dk, (h + 1) * dk) for h in heads]
        vls = [slice(h * dv, (h + 1) * dv) for h in heads]
        sup_rows = [pl.ds(pl.multiple_of(base + p * sup, chunk), sup) for p in range(tt // sup)]
        span_rows = [pl.ds(pl.multiple_of(base + c * span, chunk), span) for c in range(sps)]
        sc = [[_dot_nt(qm_scr[r, kls[h]], km_scr[r, kls[h]]) for r in sup_rows] for h in heads]
        if pair == 2:
            nb = [[_dot_nt(qn_scr[r, kls[h]], kn_scr[r, kls[h]]) for r in sup_rows] for h in heads]
            sc = [[jnp.where(below_diag, nb[h][p], sc[h][p]) for p in range(len(sup_rows))] for h in heads]
            wanted = diag_causal | below_diag
        else:
            wanted = diag_causal
        upd = [[_dot_tn(v_scr[r, vls[h]], ke_scr[r, kls[h]]) for r in span_rows] for h in heads]
        intra = [[_dot(jnp.where(wanted, sc[h][p], 0.0).astype(BF16), v_scr[r, vls[h]])
                  for p, r in enumerate(sup_rows)] for h in heads]
        states = []
        for h in heads:
            st = st_ref[s, h]
            at_span_start = []
            for c in range(sps):
                at_span_start.append(st.astype(BF16))
                st = st * dec_scr[s * sps + c, :, kls[h]] + upd[h][c]
            st_ref[s, h] = st
            states.append(at_span_start)
        for h in heads:
            for c, r in enumerate(span_rows):
                p, off = divmod(c * span, sup)
                o = intra[h][p][off:off + span] + _dot_nt(qd_scr[r, kls[h]], states[h][c])
                on_scr[r, vls[h]] = o * lax.rsqrt(jnp.mean(o * o, axis=-1, keepdims=True) + EPS)

    def chain(grp):
        r0 = grp * gn
        x = x_ref[grp * gs:(grp + 1) * gs].reshape(gn, x_ref.shape[-1])
        hn = _rms(x, g_ref[...]).astype(BF16)
        proj = _dot(hn, win_ref[...])
        fr = proj[:, hdim:2 * hdim]
        gate = proj[:, 2 * hdim + vdim:]
        v_scr[r0:r0 + gn, :] = proj[:, 2 * hdim:2 * hdim + vdim].astype(BF16)
        yield

        f = lb + (1.0 - lb) * _sigmoid(fr)
        logf = jnp.log(f)
        qs = proj[:, :hdim]
        q = _silu(qs)
        k = 1.0 - f
        hi = logf.astype(BF16)
        lo = (logf - hi.astype(F32)).astype(BF16)
        tri = tri_scr[...]
        b = _dot(tri, hi) + _dot(tri, lo)
        for c2 in range(gn // span):
            parts = []
            for c in range(c2 * pair, (c2 + 1) * pair):
                sl = slice(c * chunk, (c + 1) * chunk)
                dst = slice(r0 + c * chunk, r0 + (c + 1) * chunk)
                bc = b[sl]
                mid = bc[half - 1:half, :]
                last = bc[chunk - 1:chunk, :]
                q_mid = q[sl] * jnp.exp(bc - mid)
                k_mid = k[sl] * jnp.exp(mid - bc)
                qm_scr[dst, :] = q_mid.astype(BF16)
                km_scr[dst, :] = k_mid.astype(BF16)
                parts.append((dst, q_mid * jnp.exp(mid), k_mid * jnp.exp(last - mid), jnp.exp(last)))
            if pair == 1:
                (dst, q_in, k_out, dec), = parts
                qd_scr[dst, :] = q_in.astype(BF16)
                ke_scr[dst, :] = k_out.astype(BF16)
            else:
                (dst_a, q_in_a, k_out_a, dec_a), (dst_b, q_in_b, k_out_b, dec_b) = parts
                qn_scr[dst_a, :] = q_in_a.astype(BF16)
                qn_scr[dst_b, :] = q_in_b.astype(BF16)
                kn_scr[dst_a, :] = k_out_a.astype(BF16)
                kn_scr[dst_b, :] = k_out_b.astype(BF16)
                qd_scr[dst_a, :] = q_in_a.astype(BF16)
                qd_scr[dst_b, :] = (q_in_b * dec_a).astype(BF16)
                ke_scr[dst_a, :] = (k_out_a * dec_b).astype(BF16)
                ke_scr[dst_b, :] = k_out_b.astype(BF16)
                dec = dec_a * dec_b
            dec_scr[r0 // span + c2] = dec
        yield

        _for_each(gs, lambda i: recurrence(grp * gs + i))
        yield

        y = on_scr[r0:r0 + gn, :] * on_ref[...] * _silu(gate)
        o_ref[grp * gs:(grp + 1) * gs] = (x + _dot(y.astype(BF16), wout_ref[...])).reshape(gs, tt, x.shape[-1])

    _run_interleaved([chain(grp) for grp in range(bb // gs)])

    @pl.when(t == nt - 1)
    def _():
        for s in range(bb):
            for h in range(HG_HEADS):
                s_ref[s, h] = st_ref[s, h].T


def _hgrn(x, s0, shared_init, g, w_in, a_lb, onorm, w_out, lead, *, bb, gs, tt, chunk, pair, layer_j):
    bsz, tlen, d = x.shape
    _, heads, dk, dv = s0.shape
    n = bb * tt
    shared = shared_init is not None
    kern = functools.partial(_hgrn_kernel, bb=bb, gs=gs, tt=tt, chunk=chunk, pair=pair, layer_j=layer_j, dk=dk, dv=dv,
                             shared_init=shared)
    if shared:
        s0_spec = pl.BlockSpec((1, heads, dk, dv), lambda b, t: (shared_init, 0, 0, 0))
    else:
        s0_spec = pl.BlockSpec((bb, heads, dk, dv), lambda b, t: (b, 0, 0, 0))
    return pl.pallas_call(
        kern,
        grid=(bsz // bb, tlen // tt),
        in_specs=[
            pl.BlockSpec((bb, tt, d), lambda b, t: (b, t, 0)),
            s0_spec,
            _const_spec((1, d)),
            _pick_spec(w_in.shape, lead),
            _const_spec(a_lb.shape),
            _const_spec((1, heads * dv)),
            _pick_spec(w_out.shape, lead),
        ],
        out_specs=[
            pl.BlockSpec((bb, tt, d), lambda b, t: (b, t, 0)),
            pl.BlockSpec((bb, heads, dk, dv), lambda b, t: (b, 0, 0, 0)),
        ],
        out_shape=[
            jax.ShapeDtypeStruct((bsz, tlen, d), F32),
            jax.ShapeDtypeStruct((bsz, heads, dk, dv), F32),
        ],
        scratch_shapes=[
            pltpu.VMEM((bb, heads, dv, dk), F32),
            pltpu.VMEM((n, heads * dk), BF16),
            pltpu.VMEM((n, heads * dk), BF16),
            pltpu.VMEM((n, heads * dk), BF16),
            pltpu.VMEM((n, heads * dk), BF16),
            pltpu.VMEM((n, heads * dk) if pair == 2 else (SUBLANES, LANES), BF16),
            pltpu.VMEM((n, heads * dk) if pair == 2 else (SUBLANES, LANES), BF16),
            pltpu.VMEM((n, heads * dv), BF16),
            pltpu.VMEM((n // (chunk * pair), 1, heads * dk), F32),
            pltpu.VMEM((n, heads * dv), F32),
            pltpu.VMEM((gs * tt, gs * tt), BF16),
        ],
        compiler_params=pltpu.CompilerParams(
            dimension_semantics=("arbitrary", "arbitrary"), vmem_limit_bytes=VMEM_LIMIT_BYTES),
        name="hgrn2",
    )(x, s0, g.reshape(1, d), w_in, a_lb, onorm.reshape(1, heads * dv), w_out)


def _segment_perm(n, tt, transpose):
    seg = tt // SUBLANES
    r = lax.broadcasted_iota(jnp.int32, (n, n), 1 if transpose else 0)
    c = lax.broadcasted_iota(jnp.int32, (n, n), 0 if transpose else 1)
    loc = r % tt
    src = (r - loc) + (loc % SUBLANES) * seg + loc // SUBLANES
    return jnp.where(c == src, 1.0, 0.0).astype(BF16)


def _rglru_kernel(x_ref, h0_ref, c0_ref, g_ref, win_ref, cw_ref, cb_ref, wax_ref,
                  ba_ref, bx_ref, lam_ref, wout_ref,
                  o_ref, h_ref, c_ref,
                  xb_scr, hs_scr, to_seg_scr, from_seg_scr,
                  *, bb, gs, tt, d_rnn, shared_init):
    t = pl.program_id(1)
    gn = gs * tt
    bw = d_rnn // RG_BLOCKS
    tail = CONV_W - 1
    seg = tt // SUBLANES

    @pl.when(t == 0)
    def _():
        for s in range(bb):
            h_ref[s] = h0_ref[0 if shared_init else s]
            c_ref[s] = c0_ref[0 if shared_init else s]
        to_seg_scr[...] = _segment_perm(gn, tt, False)
        from_seg_scr[...] = _segment_perm(gn, tt, True)

    cw = cw_ref[...]
    softplus_neg_lam = jax.nn.softplus(-lam_ref[...])
    sub = lax.broadcasted_iota(jnp.int32, (SUBLANES, d_rnn), 0)

    def stream_pieces(s):
        base = s * tt
        xg = [xb_scr[pl.ds(pl.multiple_of(base + g * SUBLANES, SUBLANES), SUBLANES), :] for g in range(seg)]
        prev = c_ref[s]
        h0 = h_ref[s]

        def delayed(g, j):
            gg, wraps = g - j, 0
            while gg < 0:
                gg, wraps = gg + seg, wraps + 1
            v = xg[gg]
            if wraps:
                v = pltpu.roll(v, wraps, 0)
                for sl in range(wraps):
                    i = tail + sl * seg + g - j
                    v = jnp.where(sub == sl, prev[i:i + 1, :], v)
            return v

        conv = []
        for g in range(seg):
            acc = cb_ref[...] + delayed(g, tail) * cw[0:1, :]
            for j in range(1, CONV_W):
                acc = acc + delayed(g, tail - j) * cw[j:j + 1, :]
            conv.append(acc)
        for i in range(tail):
            step = tt - tail + i
            c_ref[s, i:i + 1, :] = xg[step % seg][step // seg:step // seg + 1, :]
        conv = jnp.concatenate(conv, axis=0)
        yield

        cfb = conv.astype(BF16)
        pre = [_dot(cfb[:, i * bw:(i + 1) * bw], wax_ref[i]) for i in range(RG_BLOCKS)]
        r = _sigmoid(jnp.concatenate([p[:, :bw] for p in pre], axis=-1) + ba_ref[...])
        ig = _sigmoid(jnp.concatenate([p[:, bw:] for p in pre], axis=-1) + bx_ref[...])
        yield
        log_a = -RG_C * r * softplus_neg_lam
        a = jnp.exp(log_a)
        u = jnp.sqrt(-jnp.tanh(log_a) * (a * a + 1.0)) * (ig * conv)
        yield

        hz = [u[0:SUBLANES]]
        az = [a[0:SUBLANES]]
        for g in range(1, seg):
            ag = a[g * SUBLANES:(g + 1) * SUBLANES]
            hz.append(ag * hz[-1] + u[g * SUBLANES:(g + 1) * SUBLANES])
            az.append(ag * az[-1])
        e_end, a_end = hz[-1], az[-1]
        d = 1
        while d < SUBLANES:
            keep = sub >= d
            e_sh = pltpu.roll(e_end, d, 0)
            a_sh = pltpu.roll(a_end, d, 0)
            e_end = jnp.where(keep, a_end * e_sh + e_end, e_end)
            a_end = jnp.where(keep, a_end * a_sh, a_end)
            d *= 2
        h_end = a_end * h0 + e_end
        h_start = jnp.where(sub == 0, h0, pltpu.roll(h_end, 1, 0))
        h_ref[s] = h_end[SUBLANES - 1:SUBLANES, :]
        for g in range(seg):
            rows = pl.ds(pl.multiple_of(base + g * SUBLANES, SUBLANES), SUBLANES)
            hs_scr[rows, :] = hz[g] + az[g] * h_start

    def chain(grp):
        r0 = grp * gn
        x = x_ref[grp * gs:(grp + 1) * gs].reshape(gn, x_ref.shape[-1])
        hn = _rms(x, g_ref[...]).astype(BF16)
        hp = _dot(to_seg_scr[...], hn).astype(BF16)
        yield
        xb_scr[r0:r0 + gn, :] = _dot(hp, win_ref[:, :d_rnn])
        yield
        gb = _dot(hp, win_ref[:, d_rnn:])
        yield

        if gs == 1:
            yield from stream_pieces(grp)
        else:
            _for_each(gs, lambda i: list(stream_pieces(grp * gs + i)))
        yield

        y = (hs_scr[r0:r0 + gn, :] * jax.nn.gelu(gb)).astype(BF16)
        yield
        yn = _dot(from_seg_scr[...], y).astype(BF16)
        o_ref[grp * gs:(grp + 1) * gs] = (x + _dot(yn, wout_ref[...])).reshape(gs, tt, x.shape[-1])

    _run_interleaved([chain(grp) for grp in range(bb // gs)])


def _rglru(x, h0, c0, shared_init, g, w_in, conv_w, conv_b, wax, ba, bx, lam, w_out, lead, *, bb, gs, tt):
    bsz, tlen, d = x.shape
    d_rnn = h0.shape[-1]
    tail = c0.shape[1]
    n = bb * tt
    shared = shared_init is not None
    kern = functools.partial(_rglru_kernel, bb=bb, gs=gs, tt=tt, d_rnn=d_rnn, shared_init=shared)
    row = lambda v: v.reshape(1, d_rnn)
    if shared:
        init_map, init_n = (lambda b, t: (shared_init, 0, 0)), 1
    else:
        init_map, init_n = (lambda b, t: (b, 0, 0)), bb
    return pl.pallas_call(
        kern,
        grid=(bsz // bb, tlen // tt),
        in_specs=[
            pl.BlockSpec((bb, tt, d), lambda b, t: (b, t, 0)),
            pl.BlockSpec((init_n, 1, d_rnn), init_map),
            pl.BlockSpec((init_n, tail, d_rnn), init_map),
            _const_spec((1, d)),
            _pick_spec(w_in.shape, lead),
            _const_spec(conv_w.shape),
            _const_spec((1, d_rnn)),
            _pick_spec(wax.shape, lead),
            _const_spec((1, d_rnn)),
            _const_spec((1, d_rnn)),
            _const_spec((1, d_rnn)),
            _pick_spec(w_out.shape, lead),
        ],
        out_specs=[
            pl.BlockSpec((bb, tt, d), lambda b, t: (b, t, 0)),
            pl.BlockSpec((bb, 1, d_rnn), lambda b, t: (b, 0, 0)),
            pl.BlockSpec((bb, tail, d_rnn), lambda b, t: (b, 0, 0)),
        ],
        out_shape=[
            jax.ShapeDtypeStruct((bsz, tlen, d), F32),
            jax.ShapeDtypeStruct((bsz, 1, d_rnn), F32),
            jax.ShapeDtypeStruct((bsz, tail, d_rnn), F32),
        ],
        scratch_shapes=[
            pltpu.VMEM((n, d_rnn), F32),
            pltpu.VMEM((n, d_rnn), F32),
            pltpu.VMEM((gs * tt, gs * tt), BF16),
            pltpu.VMEM((gs * tt, gs * tt), BF16),
        ],
        compiler_params=pltpu.CompilerParams(
            dimension_semantics=("arbitrary", "arbitrary"), vmem_limit_bytes=VMEM_LIMIT_BYTES),
        name="rglru",
    )(x, h0, c0, g.reshape(1, d), w_in, conv_w, row(conv_b), wax, row(ba), row(bx), row(lam), w_out)


def _trunk(x, s_hgrn, s_h, s_conv, shared_init, w, ffn, *, hgrn_tiles, rglru_tiles):
    bsz, tlen, d = x.shape
    depth = w["mix_norm"].shape[0]
    new_s, new_h, new_c = [], [], []
    h = x.reshape(bsz * tlen, d)
    for layer in range(depth):
        j = layer // N_MIXERS
        h = ffn(h, layer, 0, False)
        h3 = h.reshape(bsz, tlen, d)
        if layer % N_MIXERS == 0:
            h3, s_new = _hgrn(h3, s_hgrn[j], shared_init, w["mix_norm"][layer], w["a_w_in"], w["a_lb"],
                              w["a_onorm"][j], w["a_w_out"], (j,), layer_j=j, **hgrn_tiles)
            new_s.append(s_new)
        else:
            h3, h_new, c_new = _rglru(h3, s_h[j], s_conv[j], shared_init, w["mix_norm"][layer], w["b_w_in"],
                                      w["b_conv_w"][j], w["b_conv_b"][j], w["b_wax"], w["b_ba"][j],
                                      w["b_bx"][j], w["b_lambda"][j], w["b_w_out"], (j,), **rglru_tiles)
            new_h.append(h_new)
            new_c.append(c_new)
        h = h3.reshape(bsz * tlen, d)
        h = ffn(h, layer, 1, layer == depth - 1)
    return h.reshape(bsz, tlen, d), new_s, new_h, new_c


def kernel(x_prompt, x_sample, state_hgrn, state_rglru, state_conv, meta_tokens, ffn_norm, ffn_w_in, ffn_w_out, mix_norm, a_w_in, a_lb, a_onorm, a_w_out, b_w_in, b_conv_w, b_conv_b, b_wa, b_ba, b_wx, b_bx, b_lambda, b_w_out, final_norm):
    n_dec, dec_seq, d = x_sample.shape
    n_meta = meta_tokens.shape[0]
    assert n_meta == dec_seq, "the meta prefix is run as one more stream of the short pass"

    w = dict(
        mix_norm=mix_norm,
        a_w_in=a_w_in.astype(BF16), a_lb=a_lb, a_onorm=a_onorm, a_w_out=a_w_out.astype(BF16),
        b_w_in=b_w_in.astype(BF16), b_conv_w=b_conv_w, b_conv_b=b_conv_b,
        b_wax=jnp.concatenate([b_wa, b_wx], axis=-1).astype(BF16),
        b_ba=b_ba, b_bx=b_bx, b_lambda=b_lambda, b_w_out=b_w_out.astype(BF16),
    )

    def with_zero_stream(s):
        return jnp.concatenate([s, jnp.zeros_like(s[:, :1])], axis=1)

    xs = jnp.concatenate([x_sample, meta_tokens.astype(x_sample.dtype)[None]], axis=0)
    n_short = n_dec + 1

    ffn_bf16 = {}

    def ffn_short(h, layer, slot, last):
        h, *ffn_bf16[layer, slot] = _ffn_cast(
            h, ffn_norm[layer, slot], ffn_w_in, ffn_w_out, (layer, slot), final_norm,
            final_norm=last)
        return h

    def ffn_long(h, layer, slot, last):
        return _ffn(h, ffn_norm[layer, slot], *ffn_bf16[layer, slot], final_norm,
                    tm=1024, parts=2, final_norm=last)

    ys, s_s, h_s, c_s = _trunk(
        xs, with_zero_stream(state_hgrn), with_zero_stream(state_rglru)[:, :, None, :],
        with_zero_stream(state_conv), None, w, ffn_short,
        hgrn_tiles=dict(bb=n_short, gs=n_short, tt=dec_seq, chunk=dec_seq, pair=1),
        rglru_tiles=dict(bb=n_short, gs=n_short, tt=dec_seq))

    yp, s_p, h_p, c_p = _trunk(
        x_prompt, s_s, h_s, c_s, n_dec, w, ffn_long,
        hgrn_tiles=dict(bb=2, gs=1, tt=256, chunk=32, pair=2),
        rglru_tiles=dict(bb=4, gs=1, tt=256))

    return (yp, ys[:n_dec],
            jnp.stack(s_p), jnp.stack([s[:n_dec] for s in s_s]),
            jnp.stack([h[:, 0] for h in h_p]), jnp.stack([h[:n_dec, 0] for h in h_s]),
            jnp.stack(c_p), jnp.stack([c[:n_dec] for c in c_s]))
```

```python
import functools
import itertools

import jax
import jax.numpy as jnp
from jax import lax
from jax.experimental import pallas as pl
from jax.experimental.pallas import tpu as pltpu

F32 = jnp.float32
BF16 = jnp.bfloat16

EPS = 1e-6
FFN_RES = 0.5
RG_C = 8.0
N_MIXERS = 2
HG_HEADS = 8
RG_BLOCKS = 8
CONV_W = 4

V7X_VMEM_BYTES = 64 * 1024 * 1024
VMEM_LIMIT_BYTES = V7X_VMEM_BYTES - 8 * 1024 * 1024
SUBLANES = 8
LANES = 128
V7X_MXU_DIM = 256


def _rms(x, g):
    return x * lax.rsqrt(jnp.mean(x * x, axis=-1, keepdims=True) + EPS) * g


def _sigmoid(x):
    return 0.5 * jnp.tanh(0.5 * x) + 0.5


def _silu(x):
    h = 0.5 * x
    return h * jnp.tanh(h) + h


def _dot(a, b):
    return jnp.dot(a, b, preferred_element_type=F32)


def _dot_nt(a, b):
    return lax.dot_general(a, b, (((1,), (1,)), ((), ())), preferred_element_type=F32)


def _dot_tn(a, b):
    return lax.dot_general(a, b, (((0,), (0,)), ((), ())), preferred_element_type=F32)


def _const_spec(shape):
    zeros = (0,) * len(shape)
    return pl.BlockSpec(shape, lambda *_: zeros)


def _pick_spec(shape, lead):
    tail = shape[len(lead):]
    index = tuple(lead) + (0,) * len(tail)
    return pl.BlockSpec((None,) * len(lead) + tuple(tail), lambda *_: index)


def _for_each(count, body):
    if count == 1:
        body(0)
    else:
        lax.fori_loop(0, count, lambda i, c: (body(i), c)[1], 0)


def _run_interleaved(chains):
    for _ in itertools.zip_longest(*chains):
        pass


def _swiglu_partial(yb, wa, wb, wo):
    a = _dot(yb, wa)
    b = _dot(yb, wb)
    return _dot((_silu(a) * b).astype(BF16), wo)


def _ffn_kernel(x_ref, g_ref, wa_ref, wb_ref, wo_ref, fg_ref, o_ref, *, d_ff, fc, parts, final_norm):
    rows = x_ref.shape[0] // parts

    def chain(part):
        sl = slice(part * rows, (part + 1) * rows)
        x = x_ref[sl, :]
        yb = _rms(x, g_ref[...]).astype(BF16)
        yield
        acc = None
        for lo in range(0, d_ff, fc):
            p = _swiglu_partial(yb, wa_ref[:, lo:lo + fc], wb_ref[:, lo:lo + fc], wo_ref[lo:lo + fc, :])
            acc = p if acc is None else acc + p
            yield
        out = x + FFN_RES * acc
        if final_norm:
            out = _rms(out, fg_ref[...])
        o_ref[sl, :] = out

    chains = [itertools.chain([None] * part, chain(part)) for part in range(parts)]
    _run_interleaved(chains)


def _ffn(x, g, wa, wb, wo, fg, *, tm, parts, final_norm):
    n, d = x.shape
    d_ff = wo.shape[0]
    fc = V7X_MXU_DIM
    assert d_ff % fc == 0
    kern = functools.partial(_ffn_kernel, d_ff=d_ff, fc=fc, parts=parts, final_norm=final_norm)
    return pl.pallas_call(
        kern,
        grid=(n // tm,),
        in_specs=[
            pl.BlockSpec((tm, d), lambda i: (i, 0)),
            _const_spec((1, d)),
            _const_spec(wa.shape),
            _const_spec(wb.shape),
            _const_spec(wo.shape),
            _const_spec((1, d)),
        ],
        out_specs=pl.BlockSpec((tm, d), lambda i: (i, 0)),
        out_shape=jax.ShapeDtypeStruct((n, d), F32),
        compiler_params=pltpu.CompilerParams(
            dimension_semantics=("arbitrary",), vmem_limit_bytes=VMEM_LIMIT_BYTES),
        name="ffn",
    )(x, g.reshape(1, d), wa, wb, wo, fg.reshape(1, d))


def _ffn_cast_kernel(x_ref, g_ref, wa_ref, wb_ref, wo_ref, fg_ref,
                     o_ref, wa16_ref, wb16_ref, wo16_ref,
                     yb_scr, acc_scr, *, final_norm):
    c = pl.program_id(0)

    @pl.when(c == 0)
    def _():
        yb_scr[...] = _rms(x_ref[...], g_ref[...]).astype(BF16)
        acc_scr[...] = jnp.zeros_like(acc_scr)

    wa = wa_ref[...].astype(BF16)
    wb = wb_ref[...].astype(BF16)
    wo = wo_ref[...].astype(BF16)
    wa16_ref[...] = wa
    wb16_ref[...] = wb
    wo16_ref[...] = wo
    acc_scr[...] += _swiglu_partial(yb_scr[...], wa, wb, wo)

    @pl.when(c == pl.num_programs(0) - 1)
    def _():
        out = x_ref[...] + FFN_RES * acc_scr[...]
        if final_norm:
            out = _rms(out, fg_ref[...])
        o_ref[...] = out


def _ffn_cast(x, g, w_in, w_out, lead, fg, *, final_norm):
    n, d = x.shape
    d_ff = w_out.shape[-2]
    fc = V7X_MXU_DIM
    steps = d_ff // fc
    assert d_ff % fc == 0
    none = (None,) * len(lead)
    kern = functools.partial(_ffn_cast_kernel, final_norm=final_norm)
    return pl.pallas_call(
        kern,
        grid=(steps,),
        in_specs=[
            _const_spec((n, d)),
            _const_spec((1, d)),
            pl.BlockSpec(none + (d, fc), lambda c: tuple(lead) + (0, c)),
            pl.BlockSpec(none + (d, fc), lambda c: tuple(lead) + (0, steps + c)),
            pl.BlockSpec(none + (fc, d), lambda c: tuple(lead) + (c, 0)),
            _const_spec((1, d)),
        ],
        out_specs=[
            _const_spec((n, d)),
            pl.BlockSpec((d, fc), lambda c: (0, c)),
            pl.BlockSpec((d, fc), lambda c: (0, c)),
            pl.BlockSpec((fc, d), lambda c: (c, 0)),
        ],
        out_shape=[
            jax.ShapeDtypeStruct((n, d), F32),
            jax.ShapeDtypeStruct((d, d_ff), BF16),
            jax.ShapeDtypeStruct((d, d_ff), BF16),
            jax.ShapeDtypeStruct((d_ff, d), BF16),
        ],
        scratch_shapes=[pltpu.VMEM((n, d), BF16), pltpu.VMEM((n, d), F32)],
        compiler_params=pltpu.CompilerParams(
            dimension_semantics=("arbitrary",), vmem_limit_bytes=VMEM_LIMIT_BYTES),
        name="ffn_cast",
    )(x, g.reshape(1, d), w_in, w_in, w_out, fg.reshape(1, d))


def _hgrn_kernel(x_ref, s0_ref, g_ref, win_ref, alb_ref, on_ref, wout_ref,
                 o_ref, s_ref,
                 st_ref, qm_scr, km_scr, qd_scr, ke_scr, qn_scr, kn_scr, v_scr, dec_scr, on_scr, tri_scr,
                 *, bb, gs, tt, chunk, pair, layer_j, dk, dv, shared_init):
    t = pl.program_id(1)
    nt = pl.num_programs(1)
    gn = gs * tt
    hdim = HG_HEADS * dk
    vdim = HG_HEADS * dv
    half = chunk // 2
    span = chunk * pair
    sps = tt // span
    sup = LANES if tt % LANES == 0 else tt
    assert pair in (1, 2) and sup % span == 0

    @pl.when(t == 0)
    def _():
        for s in range(bb):
            for h in range(HG_HEADS):
                st_ref[s, h] = s0_ref[0 if shared_init else s, h].T
        row = lax.broadcasted_iota(jnp.int32, (gn, gn), 0)
        col = lax.broadcasted_iota(jnp.int32, (gn, gn), 1)
        tri_scr[...] = jnp.where((row // chunk == col // chunk) & (col <= row), 1.0, 0.0).astype(BF16)

    alb = alb_ref[...]
    e = jnp.exp(alb - jnp.max(alb, axis=0, keepdims=True))
    lb = jnp.sum(e[:layer_j + 1], axis=0, keepdims=True) / jnp.sum(e, axis=0, keepdims=True)

    srow = lax.broadcasted_iota(jnp.int32, (sup, sup), 0)
    scol = lax.broadcasted_iota(jnp.int32, (sup, sup), 1)
    diag_causal = (srow // chunk == scol // chunk) & (scol <= srow)
    below_diag = (srow // chunk == scol // chunk + 1) & (srow // span == scol // span)

    def recurrence(s):
        base = s * tt
        heads = range(HG_HEADS)
        kls = [slice(h * dk, (h + 1) * dk) for h in heads]
        vls = [slice(h * dv, (h + 1) * dv) for h in heads]
        sup_rows = [pl.ds(pl.multiple_of(base + p * sup, chunk), sup) for p in range(tt // sup)]
        span_rows = [pl.ds(pl.multiple_of(base + c * span, chunk), span) for c in range(sps)]
        sc = [[_dot_nt(qm_scr[r, kls[h]], km_scr[r, kls[h]]) for r in sup_rows] for h in heads]
        if pair == 2:
            nb = [[_dot_nt(qn_scr[r, kls[h]], kn_scr[r, kls[h]]) for r in sup_rows] for h in heads]
            sc = [[jnp.where(below_diag, nb[h][p], sc[h][p]) for p in range(len(sup_rows))] for h in heads]
            wanted = diag_causal | below_diag
        else:
            wanted = diag_causal
        upd = [[_dot_tn(v_scr[r, vls[h]], ke_scr[r, kls[h]]) for r in span_rows] for h in heads]
        intra = [[_dot(jnp.where(wanted, sc[h][p], 0.0).astype(BF16), v_scr[r, vls[h]])
                  for p, r in enumerate(sup_rows)] for h in heads]
        states = []
        for h in heads:
            st = st_ref[s, h]
            at_span_start = []
            for c in range(sps):
                at_span_start.append(st.T.astype(BF16))
                st = st * dec_scr[s * sps + c, :, kls[h]] + upd[h][c]
            st_ref[s, h] = st
            states.append(at_span_start)
        for h in heads:
            for c, r in enumerate(span_rows):
                p, off = divmod(c * span, sup)
                o = intra[h][p][off:off + span] + _dot(qd_scr[r, kls[h]], states[h][c])
                on_scr[r, vls[h]] = o * lax.rsqrt(jnp.mean(o * o, axis=-1, keepdims=True) + EPS)

    def chain(grp):
        r0 = grp * gn
        x = x_ref[grp * gs:(grp + 1) * gs].reshape(gn, x_ref.shape[-1])
        hn = _rms(x, g_ref[...]).astype(BF16)
        proj = _dot(hn, win_ref[...])
        fr = proj[:, hdim:2 * hdim]
        gate = proj[:, 2 * hdim + vdim:]
        v_scr[r0:r0 + gn, :] = proj[:, 2 * hdim:2 * hdim + vdim].astype(BF16)
        yield

        f = lb + (1.0 - lb) * _sigmoid(fr)
        logf = jnp.log(f)
        qs = proj[:, :hdim]
        q = _silu(qs)
        k = 1.0 - f
        hi = logf.astype(BF16)
        lo = (logf - hi.astype(F32)).astype(BF16)
        tri = tri_scr[...]
        b = _dot(tri, hi) + _dot(tri, lo)
        for c2 in range(gn // span):
            parts = []
            for c in range(c2 * pair, (c2 + 1) * pair):
                sl = slice(c * chunk, (c + 1) * chunk)
                dst = slice(r0 + c * chunk, r0 + (c + 1) * chunk)
                bc = b[sl]
                mid = bc[half - 1:half, :]
                last = bc[chunk - 1:chunk, :]
                q_mid = q[sl] * jnp.exp(bc - mid)
                k_mid = k[sl] * jnp.exp(mid - bc)
                qm_scr[dst, :] = q_mid.astype(BF16)
                km_scr[dst, :] = k_mid.astype(BF16)
                parts.append((dst, q_mid * jnp.exp(mid), k_mid * jnp.exp(last - mid), jnp.exp(last)))
            if pair == 1:
                (dst, q_in, k_out, dec), = parts
                qd_scr[dst, :] = q_in.astype(BF16)
                ke_scr[dst, :] = k_out.astype(BF16)
            else:
                (dst_a, q_in_a, k_out_a, dec_a), (dst_b, q_in_b, k_out_b, dec_b) = parts
                qn_scr[dst_a, :] = q_in_a.astype(BF16)
                qn_scr[dst_b, :] = q_in_b.astype(BF16)
                kn_scr[dst_a, :] = k_out_a.astype(BF16)
                kn_scr[dst_b, :] = k_out_b.astype(BF16)
                qd_scr[dst_a, :] = q_in_a.astype(BF16)
                qd_scr[dst_b, :] = (q_in_b * dec_a).astype(BF16)
                ke_scr[dst_a, :] = (k_out_a * dec_b).astype(BF16)
                ke_scr[dst_b, :] = k_out_b.astype(BF16)
                dec = dec_a * dec_b
            dec_scr[r0 // span + c2] = dec
        yield

        _for_each(gs, lambda i: recurrence(grp * gs + i))
        yield

        y = on_scr[r0:r0 + gn, :] * on_ref[...] * _silu(gate)
        o_ref[grp * gs:(grp + 1) * gs] = (x + _dot(y.astype(BF16), wout_ref[...])).reshape(gs, tt, x.shape[-1])

    _run_interleaved([chain(grp) for grp in range(bb // gs)])

    @pl.when(t == nt - 1)
    def _():
        for s in range(bb):
            for h in range(HG_HEADS):
                s_ref[s, h] = st_ref[s, h].T


def _hgrn(x, s0, shared_init, g, w_in, a_lb, onorm, w_out, lead, *, bb, gs, tt, chunk, pair, layer_j):
    bsz, tlen, d = x.shape
    _, heads, dk, dv = s0.shape
    n = bb * tt
    shared = shared_init is not None
    kern = functools.partial(_hgrn_kernel, bb=bb, gs=gs, tt=tt, chunk=chunk, pair=pair, layer_j=layer_j, dk=dk, dv=dv,
                             shared_init=shared)
    if shared:
        s0_spec = pl.BlockSpec((1, heads, dk, dv), lambda b, t: (shared_init, 0, 0, 0))
    else:
        s0_spec = pl.BlockSpec((bb, heads, dk, dv), lambda b, t: (b, 0, 0, 0))
    return pl.pallas_call(
        kern,
        grid=(bsz // bb, tlen // tt),
        in_specs=[
            pl.BlockSpec((bb, tt, d), lambda b, t: (b, t, 0)),
            s0_spec,
            _const_spec((1, d)),
            _pick_spec(w_in.shape, lead),
            _const_spec(a_lb.shape),
            _const_spec((1, heads * dv)),
            _pick_spec(w_out.shape, lead),
        ],
        out_specs=[
            pl.BlockSpec((bb, tt, d), lambda b, t: (b, t, 0)),
            pl.BlockSpec((bb, heads, dk, dv), lambda b, t: (b, 0, 0, 0)),
        ],
        out_shape=[
            jax.ShapeDtypeStruct((bsz, tlen, d), F32),
            jax.ShapeDtypeStruct((bsz, heads, dk, dv), F32),
        ],
        scratch_shapes=[
            pltpu.VMEM((bb, heads, dv, dk), F32),
            pltpu.VMEM((n, heads * dk), BF16),
            pltpu.VMEM((n, heads * dk), BF16),
            pltpu.VMEM((n, heads * dk), BF16),
            pltpu.VMEM((n, heads * dk), BF16),
            pltpu.VMEM((n, heads * dk) if pair == 2 else (SUBLANES, LANES), BF16),
            pltpu.VMEM((n, heads * dk) if pair == 2 else (SUBLANES, LANES), BF16),
            pltpu.VMEM((n, heads * dv), BF16),
            pltpu.VMEM((n // (chunk * pair), 1, heads * dk), F32),
            pltpu.VMEM((n, heads * dv), F32),
            pltpu.VMEM((gs * tt, gs * tt), BF16),
        ],
        compiler_params=pltpu.CompilerParams(
            dimension_semantics=("arbitrary", "arbitrary"), vmem_limit_bytes=VMEM_LIMIT_BYTES),
        name="hgrn2",
    )(x, s0, g.reshape(1, d), w_in, a_lb, onorm.reshape(1, heads * dv), w_out)


def _segment_perm(n, tt, transpose):
    seg = tt // SUBLANES
    r = lax.broadcasted_iota(jnp.int32, (n, n), 1 if transpose else 0)
    c = lax.broadcasted_iota(jnp.int32, (n, n), 0 if transpose else 1)
    loc = r % tt
    src = (r - loc) + (loc % SUBLANES) * seg + loc // SUBLANES
    return jnp.where(c == src, 1.0, 0.0).astype(BF16)


def _rglru_kernel(x_ref, h0_ref, c0_ref, g_ref, win_ref, cw_ref, cb_ref, wax_ref,
                  ba_ref, bx_ref, lam_ref, wout_ref,
                  o_ref, h_ref, c_ref,
                  xb_scr, hs_scr, to_seg_scr, from_seg_scr,
                  *, bb, gs, tt, d_rnn, shared_init):
    t = pl.program_id(1)
    gn = gs * tt
    bw = d_rnn // RG_BLOCKS
    tail = CONV_W - 1
    seg = tt // SUBLANES

    @pl.when(t == 0)
    def _():
        for s in range(bb):
            h_ref[s] = h0_ref[0 if shared_init else s]
            c_ref[s] = c0_ref[0 if shared_init else s]
        to_seg_scr[...] = _segment_perm(gn, tt, False)
        from_seg_scr[...] = _segment_perm(gn, tt, True)

    cw = cw_ref[...]
    softplus_neg_lam = jax.nn.softplus(-lam_ref[...])
    sub = lax.broadcasted_iota(jnp.int32, (SUBLANES, d_rnn), 0)

    def stream_pieces(s):
        base = s * tt
        xg = [xb_scr[pl.ds(pl.multiple_of(base + g * SUBLANES, SUBLANES), SUBLANES), :] for g in range(seg)]
        prev = c_ref[s]
        h0 = h_ref[s]

        def delayed(g, j):
            gg, wraps = g - j, 0
            while gg < 0:
                gg, wraps = gg + seg, wraps + 1
            v = xg[gg]
            if wraps:
                v = pltpu.roll(v, wraps, 0)
                for sl in range(wraps):
                    i = tail + sl * seg + g - j
                    v = jnp.where(sub == sl, prev[i:i + 1, :], v)
            return v

        conv = []
        for g in range(seg):
            acc = cb_ref[...] + delayed(g, tail) * cw[0:1, :]
            for j in range(1, CONV_W):
                acc = acc + delayed(g, tail - j) * cw[j:j + 1, :]
            conv.append(acc)
        for i in range(tail):
            step = tt - tail + i
            c_ref[s, i:i + 1, :] = xg[step % seg][step // seg:step // seg + 1, :]
        conv = jnp.concatenate(conv, axis=0)
        yield

        cfb = conv.astype(BF16)
        pre = [_dot(cfb[:, i * bw:(i + 1) * bw], wax_ref[i]) for i in range(RG_BLOCKS)]
        r = _sigmoid(jnp.concatenate([p[:, :bw] for p in pre], axis=-1) + ba_ref[...])
        ig = _sigmoid(jnp.concatenate([p[:, bw:] for p in pre], axis=-1) + bx_ref[...])
        yield
        log_a = -RG_C * r * softplus_neg_lam
        a = jnp.exp(log_a)
        u = jnp.sqrt(-jnp.tanh(log_a) * (a * a + 1.0)) * (ig * conv)
        yield

        hz = [u[0:SUBLANES]]
        az = [a[0:SUBLANES]]
        for g in range(1, seg):
            ag = a[g * SUBLANES:(g + 1) * SUBLANES]
            hz.append(ag * hz[-1] + u[g * SUBLANES:(g + 1) * SUBLANES])
            az.append(ag * az[-1])
        e_end, a_end = hz[-1], az[-1]
        d = 1
        while d < SUBLANES:
            keep = sub >= d
            e_sh = pltpu.roll(e_end, d, 0)
            a_sh = pltpu.roll(a_end, d, 0)
            e_end = jnp.where(keep, a_end * e_sh + e_end, e_end)
            a_end = jnp.where(keep, a_end * a_sh, a_end)
            d *= 2
        h_end = a_end * h0 + e_end
        h_start = jnp.where(sub == 0, h0, pltpu.roll(h_end, 1, 0))
        h_ref[s] = h_end[SUBLANES - 1:SUBLANES, :]
        for g in range(seg):
            rows = pl.ds(pl.multiple_of(base + g * SUBLANES, SUBLANES), SUBLANES)
            hs_scr[rows, :] = hz[g] + az[g] * h_start

    def chain(grp):
        r0 = grp * gn
        x = x_ref[grp * gs:(grp + 1) * gs].reshape(gn, x_ref.shape[-1])
        hn = _rms(x, g_ref[...]).astype(BF16)
        hp = _dot(to_seg_scr[...], hn).astype(BF16)
        yield
        xb_scr[r0:r0 + gn, :] = _dot(hp, win_ref[:, :d_rnn])
        yield
        gb = _dot(hp, win_ref[:, d_rnn:])
        yield

        if gs == 1:
            yield from stream_pieces(grp)
        else:
            _for_each(gs, lambda i: list(stream_pieces(grp * gs + i)))
        yield

        y = (hs_scr[r0:r0 + gn, :] * jax.nn.gelu(gb)).astype(BF16)
        yield
        yn = _dot(from_seg_scr[...], y).astype(BF16)
        o_ref[grp * gs:(grp + 1) * gs] = (x + _dot(yn, wout_ref[...])).reshape(gs, tt, x.shape[-1])

    _run_interleaved([chain(grp) for grp in range(bb // gs)])


def _rglru(x, h0, c0, shared_init, g, w_in, conv_w, conv_b, wax, ba, bx, lam, w_out, lead, *, bb, gs, tt):
    bsz, tlen, d = x.shape
    d_rnn = h0.shape[-1]
    tail = c0.shape[1]
    n = bb * tt
    shared = shared_init is not None
    kern = functools.partial(_rglru_kernel, bb=bb, gs=gs, tt=tt, d_rnn=d_rnn, shared_init=shared)
    row = lambda v: v.reshape(1, d_rnn)
    if shared:
        init_map, init_n = (lambda b, t: (shared_init, 0, 0)), 1
    else:
        init_map, init_n = (lambda b, t: (b, 0, 0)), bb
    return pl.pallas_call(
        kern,
        grid=(bsz // bb, tlen // tt),
        in_specs=[
            pl.BlockSpec((bb, tt, d), lambda b, t: (b, t, 0)),
            pl.BlockSpec((init_n, 1, d_rnn), init_map),
            pl.BlockSpec((init_n, tail, d_rnn), init_map),
            _const_spec((1, d)),
            _pick_spec(w_in.shape, lead),
            _const_spec(conv_w.shape),
            _const_spec((1, d_rnn)),
            _pick_spec(wax.shape, lead),
            _const_spec((1, d_rnn)),
            _const_spec((1, d_rnn)),
            _const_spec((1, d_rnn)),
            _pick_spec(w_out.shape, lead),
        ],
        out_specs=[
            pl.BlockSpec((bb, tt, d), lambda b, t: (b, t, 0)),
            pl.BlockSpec((bb, 1, d_rnn), lambda b, t: (b, 0, 0)),
            pl.BlockSpec((bb, tail, d_rnn), lambda b, t: (b, 0, 0)),
        ],
        out_shape=[
            jax.ShapeDtypeStruct((bsz, tlen, d), F32),
            jax.ShapeDtypeStruct((bsz, 1, d_rnn), F32),
            jax.ShapeDtypeStruct((bsz, tail, d_rnn), F32),
        ],
        scratch_shapes=[
            pltpu.VMEM((n, d_rnn), F32),
            pltpu.VMEM((n, d_rnn), F32),
            pltpu.VMEM((gs * tt, gs * tt), BF16),
            pltpu.VMEM((gs * tt, gs * tt), BF16),
        ],
        compiler_params=pltpu.CompilerParams(
            dimension_semantics=("arbitrary", "arbitrary"), vmem_limit_bytes=VMEM_LIMIT_BYTES),
        name="rglru",
    )(x, h0, c0, g.reshape(1, d), w_in, conv_w, row(conv_b), wax, row(ba), row(bx), row(lam), w_out)


def _trunk(x, s_hgrn, s_h, s_conv, shared_init, w, ffn, *, hgrn_tiles, rglru_tiles):
    bsz, tlen, d = x.shape
    depth = w["mix_norm"].shape[0]
    new_s, new_h, new_c = [], [], []
    h = x.reshape(bsz * tlen, d)
    for layer in range(depth):
        j = layer // N_MIXERS
        h = ffn(h, layer, 0, False)
        h3 = h.reshape(bsz, tlen, d)
        if layer % N_MIXERS == 0:
            h3, s_new = _hgrn(h3, s_hgrn[j], shared_init, w["mix_norm"][layer], w["a_w_in"], w["a_lb"],
                              w["a_onorm"][j], w["a_w_out"], (j,), layer_j=j, **hgrn_tiles)
            new_s.append(s_new)
        else:
            h3, h_new, c_new = _rglru(h3, s_h[j], s_conv[j], shared_init, w["mix_norm"][layer], w["b_w_in"],
                                      w["b_conv_w"][j], w["b_conv_b"][j], w["b_wax"], w["b_ba"][j],
                                      w["b_bx"][j], w["b_lambda"][j], w["b_w_out"], (j,), **rglru_tiles)
            new_h.append(h_new)
            new_c.append(c_new)
        h = h3.reshape(bsz * tlen, d)
        h = ffn(h, layer, 1, layer == depth - 1)
    return h.reshape(bsz, tlen, d), new_s, new_h, new_c


def kernel(x_prompt, x_sample, state_hgrn, state_rglru, state_conv, meta_tokens, ffn_norm, ffn_w_in, ffn_w_out, mix_norm, a_w_in, a_lb, a_onorm, a_w_out, b_w_in, b_conv_w, b_conv_b, b_wa, b_ba, b_wx, b_bx, b_lambda, b_w_out, final_norm):
    n_dec, dec_seq, d = x_sample.shape
    n_meta = meta_tokens.shape[0]
    assert n_meta == dec_seq, "the meta prefix is run as one more stream of the short pass"

    w = dict(
        mix_norm=mix_norm,
        a_w_in=a_w_in.astype(BF16), a_lb=a_lb, a_onorm=a_onorm, a_w_out=a_w_out.astype(BF16),
        b_w_in=b_w_in.astype(BF16), b_conv_w=b_conv_w, b_conv_b=b_conv_b,
        b_wax=jnp.concatenate([b_wa, b_wx], axis=-1).astype(BF16),
        b_ba=b_ba, b_bx=b_bx, b_lambda=b_lambda, b_w_out=b_w_out.astype(BF16),
    )

    def with_zero_stream(s):
        return jnp.concatenate([s, jnp.zeros_like(s[:, :1])], axis=1)

    xs = jnp.concatenate([x_sample, meta_tokens.astype(x_sample.dtype)[None]], axis=0)
    n_short = n_dec + 1

    ffn_bf16 = {}

    def ffn_short(h, layer, slot, last):
        h, *ffn_bf16[layer, slot] = _ffn_cast(
            h, ffn_norm[layer, slot], ffn_w_in, ffn_w_out, (layer, slot), final_norm,
            final_norm=last)
        return h

    def ffn_long(h, layer, slot, last):
        return _ffn(h, ffn_norm[layer, slot], *ffn_bf16[layer, slot], final_norm,
                    tm=1024, parts=2, final_norm=last)

    ys, s_s, h_s, c_s = _trunk(
        xs, with_zero_stream(state_hgrn), with_zero_stream(state_rglru)[:, :, None, :],
        with_zero_stream(state_conv), None, w, ffn_short,
        hgrn_tiles=dict(bb=n_short, gs=n_short, tt=dec_seq, chunk=dec_seq, pair=1),
        rglru_tiles=dict(bb=n_short, gs=n_short, tt=dec_seq))

    yp, s_p, h_p, c_p = _trunk(
        x_prompt, s_s, h_s, c_s, n_dec, w, ffn_long,
        hgrn_tiles=dict(bb=2, gs=1, tt=256, chunk=32, pair=2),
        rglru_tiles=dict(bb=4, gs=1, tt=256))

    return (yp, ys[:n_dec],
            jnp.stack(s_p), jnp.stack([s[:n_dec] for s in s_s]),
            jnp.stack([h[:, 0] for h in h_p]), jnp.stack([h[:n_dec, 0] for h in h_s]),
            jnp.stack(c_p), jnp.stack([c[:n_dec] for c in c_s]))
```

```python
import functools
import itertools

import jax
import jax.numpy as jnp
from jax import lax
from jax.experimental import pallas as pl
from jax.experimental.pallas import tpu as pltpu

F32 = jnp.float32
BF16 = jnp.bfloat16

EPS = 1e-6
FFN_RES = 0.5
RG_C = 8.0
N_MIXERS = 2
HG_HEADS = 8
RG_BLOCKS = 8
CONV_W = 4

V7X_VMEM_BYTES = 64 * 1024 * 1024
VMEM_LIMIT_BYTES = V7X_VMEM_BYTES - 8 * 1024 * 1024
SUBLANES = 8
LANES = 128
V7X_MXU_DIM = 256


def _rms(x, g):
    return x * lax.rsqrt(jnp.mean(x * x, axis=-1, keepdims=True) + EPS) * g


def _sigmoid(x):
    return 0.5 * jnp.tanh(0.5 * x) + 0.5


def _silu(x):
    h = 0.5 * x
    return h * jnp.tanh(h) + h


def _dot(a, b):
    return jnp.dot(a, b, preferred_element_type=F32)


def _dot_nt(a, b):
    return lax.dot_general(a, b, (((1,), (1,)), ((), ())), preferred_element_type=F32)


def _dot_tn(a, b):
    return lax.dot_general(a, b, (((0,), (0,)), ((), ())), preferred_element_type=F32)


def _const_spec(shape):
    zeros = (0,) * len(shape)
    return pl.BlockSpec(shape, lambda *_: zeros)


def _pick_spec(shape, lead):
    tail = shape[len(lead):]
    index = tuple(lead) + (0,) * len(tail)
    return pl.BlockSpec((None,) * len(lead) + tuple(tail), lambda *_: index)


def _for_each(count, body):
    if count == 1:
        body(0)
    else:
        lax.fori_loop(0, count, lambda i, c: (body(i), c)[1], 0)


def _run_interleaved(chains):
    for _ in itertools.zip_longest(*chains):
        pass


def _swiglu_partial(yb, wa, wb, wo):
    a = _dot(yb, wa)
    b = _dot(yb, wb)
    return _dot((_silu(a) * b).astype(BF16), wo)


def _ffn_kernel(x_ref, g_ref, wa_ref, wb_ref, wo_ref, fg_ref, o_ref, *, d_ff, fc, parts, final_norm):
    rows = x_ref.shape[0] // parts

    def chain(part):
        sl = slice(part * rows, (part + 1) * rows)
        x = x_ref[sl, :]
        yb = _rms(x, g_ref[...]).astype(BF16)
        yield
        acc = None
        for lo in range(0, d_ff, fc):
            p = _swiglu_partial(yb, wa_ref[:, lo:lo + fc], wb_ref[:, lo:lo + fc], wo_ref[lo:lo + fc, :])
            acc = p if acc is None else acc + p
            yield
        out = x + FFN_RES * acc
        if final_norm:
            out = _rms(out, fg_ref[...])
        o_ref[sl, :] = out

    chains = [itertools.chain([None] * part, chain(part)) for part in range(parts)]
    _run_interleaved(chains)


def _ffn(x, g, wa, wb, wo, fg, *, tm, parts, final_norm):
    n, d = x.shape
    d_ff = wo.shape[0]
    fc = V7X_MXU_DIM
    assert d_ff % fc == 0
    kern = functools.partial(_ffn_kernel, d_ff=d_ff, fc=fc, parts=parts, final_norm=final_norm)
    return pl.pallas_call(
        kern,
        grid=(n // tm,),
        in_specs=[
            pl.BlockSpec((tm, d), lambda i: (i, 0)),
            _const_spec((1, d)),
            _const_spec(wa.shape),
            _const_spec(wb.shape),
            _const_spec(wo.shape),
            _const_spec((1, d)),
        ],
        out_specs=pl.BlockSpec((tm, d), lambda i: (i, 0)),
        out_shape=jax.ShapeDtypeStruct((n, d), F32),
        compiler_params=pltpu.CompilerParams(
            dimension_semantics=("arbitrary",), vmem_limit_bytes=VMEM_LIMIT_BYTES),
        name="ffn",
    )(x, g.reshape(1, d), wa, wb, wo, fg.reshape(1, d))


def _ffn_cast_kernel(x_ref, g_ref, wa_ref, wb_ref, wo_ref, fg_ref,
                     o_ref, wa16_ref, wb16_ref, wo16_ref,
                     yb_scr, acc_scr, *, final_norm):
    c = pl.program_id(0)

    @pl.when(c == 0)
    def _():
        yb_scr[...] = _rms(x_ref[...], g_ref[...]).astype(BF16)
        acc_scr[...] = jnp.zeros_like(acc_scr)

    wa = wa_ref[...].astype(BF16)
    wb = wb_ref[...].astype(BF16)
    wo = wo_ref[...].astype(BF16)
    wa16_ref[...] = wa
    wb16_ref[...] = wb
    wo16_ref[...] = wo
    acc_scr[...] += _swiglu_partial(yb_scr[...], wa, wb, wo)

    @pl.when(c == pl.num_programs(0) - 1)
    def _():
        out = x_ref[...] + FFN_RES * acc_scr[...]
        if final_norm:
            out = _rms(out, fg_ref[...])
        o_ref[...] = out


def _ffn_cast(x, g, w_in, w_out, lead, fg, *, final_norm):
    n, d = x.shape
    d_ff = w_out.shape[-2]
    fc = V7X_MXU_DIM
    steps = d_ff // fc
    assert d_ff % fc == 0
    none = (None,) * len(lead)
    kern = functools.partial(_ffn_cast_kernel, final_norm=final_norm)
    return pl.pallas_call(
        kern,
        grid=(steps,),
        in_specs=[
            _const_spec((n, d)),
            _const_spec((1, d)),
            pl.BlockSpec(none + (d, fc), lambda c: tuple(lead) + (0, c)),
            pl.BlockSpec(none + (d, fc), lambda c: tuple(lead) + (0, steps + c)),
            pl.BlockSpec(none + (fc, d), lambda c: tuple(lead) + (c, 0)),
            _const_spec((1, d)),
        ],
        out_specs=[
            _const_spec((n, d)),
            pl.BlockSpec((d, fc), lambda c: (0, c)),
            pl.BlockSpec((d, fc), lambda c: (0, c)),
            pl.BlockSpec((fc, d), lambda c: (c, 0)),
        ],
        out_shape=[
            jax.ShapeDtypeStruct((n, d), F32),
            jax.ShapeDtypeStruct((d, d_ff), BF16),
            jax.ShapeDtypeStruct((d, d_ff), BF16),
            jax.ShapeDtypeStruct((d_ff, d), BF16),
        ],
        scratch_shapes=[pltpu.VMEM((n, d), BF16), pltpu.VMEM((n, d), F32)],
        compiler_params=pltpu.CompilerParams(
            dimension_semantics=("arbitrary",), vmem_limit_bytes=VMEM_LIMIT_BYTES),
        name="ffn_cast",
    )(x, g.reshape(1, d), w_in, w_in, w_out, fg.reshape(1, d))


def _hgrn_kernel(x_ref, s0_ref, g_ref, win_ref, alb_ref, on_ref, wout_ref,
                 o_ref, s_ref,
                 st_ref, qm_scr, km_scr, qd_scr, ke_scr, qn_scr, kn_scr, v_scr, dec_scr, on_scr, tri_scr,
                 *, bb, gs, tt, chunk, pair, keys_t, layer_j, dk, dv, shared_init):
    t = pl.program_id(1)
    nt = pl.num_programs(1)
    gn = gs * tt
    hdim = HG_HEADS * dk
    vdim = HG_HEADS * dv
    half = chunk // 2
    span = chunk * pair
    sps = tt // span
    sup = LANES if tt % LANES == 0 else tt
    assert pair in (1, 2) and sup % span == 0
    assert not keys_t or (sup == LANES and gs == 1)

    @pl.when(t == 0)
    def _():
        for s in range(bb):
            for h in range(HG_HEADS):
                st_ref[s, h] = s0_ref[0 if shared_init else s, h].T
        row = lax.broadcasted_iota(jnp.int32, (gn, gn), 0)
        col = lax.broadcasted_iota(jnp.int32, (gn, gn), 1)
        tri_scr[...] = jnp.where((row // chunk == col // chunk) & (col <= row), 1.0, 0.0).astype(BF16)

    alb = alb_ref[...]
    e = jnp.exp(alb - jnp.max(alb, axis=0, keepdims=True))
    lb = jnp.sum(e[:layer_j + 1], axis=0, keepdims=True) / jnp.sum(e, axis=0, keepdims=True)

    srow = lax.broadcasted_iota(jnp.int32, (sup, sup), 0)
    scol = lax.broadcasted_iota(jnp.int32, (sup, sup), 1)
    diag_causal = (srow // chunk == scol // chunk) & (scol <= srow)
    below_diag = (srow // chunk == scol // chunk + 1) & (srow // span == scol // span)

    def recurrence(s):
        base = s * tt
        heads = range(HG_HEADS)
        kls = [slice(h * dk, (h + 1) * dk) for h in heads]
        vls = [slice(h * dv, (h + 1) * dv) for h in heads]
        sup_rows = [pl.ds(pl.multiple_of(base + p * sup, chunk), sup) for p in range(tt // sup)]
        span_rows = [pl.ds(pl.multiple_of(base + c * span, chunk), span) for c in range(sps)]
        if keys_t:
            scores = lambda q_scr, kt_scr, h, r: _dot(q_scr[r, kls[h]], kt_scr[kls[h], r])
        else:
            scores = lambda q_scr, k_scr, h, r: _dot_nt(q_scr[r, kls[h]], k_scr[r, kls[h]])
        sc = [[scores(qm_scr, km_scr, h, r) for r in sup_rows] for h in heads]
        if pair == 2:
            nb = [[scores(qn_scr, kn_scr, h, r) for r in sup_rows] for h in heads]
            sc = [[jnp.where(below_diag, nb[h][p], sc[h][p]) for p in range(len(sup_rows))] for h in heads]
            wanted = diag_causal | below_diag
        else:
            wanted = diag_causal
        upd = [[_dot_tn(v_scr[r, vls[h]], ke_scr[r, kls[h]]) for r in span_rows] for h in heads]
        intra = [[_dot(jnp.where(wanted, sc[h][p], 0.0).astype(BF16), v_scr[r, vls[h]])
                  for p, r in enumerate(sup_rows)] for h in heads]
        states = []
        for h in heads:
            st = st_ref[s, h]
            at_span_start = []
            for c in range(sps):
                at_span_start.append(st.T.astype(BF16))
                st = st * dec_scr[s * sps + c, :, kls[h]] + upd[h][c]
            st_ref[s, h] = st
            states.append(at_span_start)
        for h in heads:
            for c, r in enumerate(span_rows):
                p, off = divmod(c * span, sup)
                o = intra[h][p][off:off + span] + _dot(qd_scr[r, kls[h]], states[h][c])
                on_scr[r, vls[h]] = o * lax.rsqrt(jnp.mean(o * o, axis=-1, keepdims=True) + EPS)

    def chain(grp):
        r0 = grp * gn
        x = x_ref[grp * gs:(grp + 1) * gs].reshape(gn, x_ref.shape[-1])
        hn = _rms(x, g_ref[...]).astype(BF16)
        proj = _dot(hn, win_ref[...])
        fr = proj[:, hdim:2 * hdim]
        gate = proj[:, 2 * hdim + vdim:]
        v_scr[r0:r0 + gn, :] = proj[:, 2 * hdim:2 * hdim + vdim].astype(BF16)
        yield

        f = lb + (1.0 - lb) * _sigmoid(fr)
        logf = jnp.log(f)
        qs = proj[:, :hdim]
        q = _silu(qs)
        k = 1.0 - f
        hi = logf.astype(BF16)
        lo = (logf - hi.astype(F32)).astype(BF16)
        tri = tri_scr[...]
        b = _dot(tri, hi) + _dot(tri, lo)
        pending = {}

        def key_rows(k_scr, dst, rows):
            if not keys_t:
                k_scr[dst, :] = rows.astype(BF16)
                return
            got = pending.setdefault(id(k_scr), [])
            got.append(rows)
            if len(got) * chunk == sup:
                block = jnp.concatenate(got, axis=0)
                cols = slice(dst.stop - sup, dst.stop)
                for h in range(HG_HEADS):
                    k_scr[h * dk:(h + 1) * dk, cols] = block[:, h * dk:(h + 1) * dk].T.astype(BF16)
                got.clear()

        for c2 in range(gn // span):
            parts = []
            for c in range(c2 * pair, (c2 + 1) * pair):
                sl = slice(c * chunk, (c + 1) * chunk)
                dst = slice(r0 + c * chunk, r0 + (c + 1) * chunk)
                bc = b[sl]
                mid = bc[half - 1:half, :]
                last = bc[chunk - 1:chunk, :]
                q_mid = q[sl] * jnp.exp(bc - mid)
                k_mid = k[sl] * jnp.exp(mid - bc)
                qm_scr[dst, :] = q_mid.astype(BF16)
                key_rows(km_scr, dst, k_mid)
                parts.append((dst, q_mid * jnp.exp(mid), k_mid * jnp.exp(last - mid), jnp.exp(last)))
            if pair == 1:
                (dst, q_in, k_out, dec), = parts
                qd_scr[dst, :] = q_in.astype(BF16)
                ke_scr[dst, :] = k_out.astype(BF16)
            else:
                (dst_a, q_in_a, k_out_a, dec_a), (dst_b, q_in_b, k_out_b, dec_b) = parts
                qn_scr[dst_a, :] = q_in_a.astype(BF16)
                qn_scr[dst_b, :] = q_in_b.astype(BF16)
                key_rows(kn_scr, dst_a, k_out_a)
                key_rows(kn_scr, dst_b, k_out_b)
                qd_scr[dst_a, :] = q_in_a.astype(BF16)
                qd_scr[dst_b, :] = (q_in_b * dec_a).astype(BF16)
                ke_scr[dst_a, :] = (k_out_a * dec_b).astype(BF16)
                ke_scr[dst_b, :] = k_out_b.astype(BF16)
                dec = dec_a * dec_b
            dec_scr[r0 // span + c2] = dec
        yield

        _for_each(gs, lambda i: recurrence(grp * gs + i))
        yield

        y = on_scr[r0:r0 + gn, :] * on_ref[...] * _silu(gate)
        o_ref[grp * gs:(grp + 1) * gs] = (x + _dot(y.astype(BF16), wout_ref[...])).reshape(gs, tt, x.shape[-1])

    _run_interleaved([chain(grp) for grp in range(bb // gs)])

    @pl.when(t == nt - 1)
    def _():
        for s in range(bb):
            for h in range(HG_HEADS):
                s_ref[s, h] = st_ref[s, h].T


def _hgrn(x, s0, shared_init, g, w_in, a_lb, onorm, w_out, lead, *, bb, gs, tt, chunk, pair, keys_t, layer_j):
    bsz, tlen, d = x.shape
    _, heads, dk, dv = s0.shape
    n = bb * tt
    shared = shared_init is not None
    key_shape = (heads * dk, n) if keys_t else (n, heads * dk)
    kern = functools.partial(_hgrn_kernel, bb=bb, gs=gs, tt=tt, chunk=chunk, pair=pair, keys_t=keys_t,
                             layer_j=layer_j, dk=dk, dv=dv,
                             shared_init=shared)
    if shared:
        s0_spec = pl.BlockSpec((1, heads, dk, dv), lambda b, t: (shared_init, 0, 0, 0))
    else:
        s0_spec = pl.BlockSpec((bb, heads, dk, dv), lambda b, t: (b, 0, 0, 0))
    return pl.pallas_call(
        kern,
        grid=(bsz // bb, tlen // tt),
        in_specs=[
            pl.BlockSpec((bb, tt, d), lambda b, t: (b, t, 0)),
            s0_spec,
            _const_spec((1, d)),
            _pick_spec(w_in.shape, lead),
            _const_spec(a_lb.shape),
            _const_spec((1, heads * dv)),
            _pick_spec(w_out.shape, lead),
        ],
        out_specs=[
            pl.BlockSpec((bb, tt, d), lambda b, t: (b, t, 0)),
            pl.BlockSpec((bb, heads, dk, dv), lambda b, t: (b, 0, 0, 0)),
        ],
        out_shape=[
            jax.ShapeDtypeStruct((bsz, tlen, d), F32),
            jax.ShapeDtypeStruct((bsz, heads, dk, dv), F32),
        ],
        scratch_shapes=[
            pltpu.VMEM((bb, heads, dv, dk), F32),
            pltpu.VMEM((n, heads * dk), BF16),
            pltpu.VMEM(key_shape, BF16),
            pltpu.VMEM((n, heads * dk), BF16),
            pltpu.VMEM((n, heads * dk), BF16),
            pltpu.VMEM((n, heads * dk) if pair == 2 else (SUBLANES, LANES), BF16),
            pltpu.VMEM(key_shape if pair == 2 else (SUBLANES, LANES), BF16),
            pltpu.VMEM((n, heads * dv), BF16),
            pltpu.VMEM((n // (chunk * pair), 1, heads * dk), F32),
            pltpu.VMEM((n, heads * dv), F32),
            pltpu.VMEM((gs * tt, gs * tt), BF16),
        ],
        compiler_params=pltpu.CompilerParams(
            dimension_semantics=("arbitrary", "arbitrary"), vmem_limit_bytes=VMEM_LIMIT_BYTES),
        name="hgrn2",
    )(x, s0, g.reshape(1, d), w_in, a_lb, onorm.reshape(1, heads * dv), w_out)


def _segment_perm(n, tt, transpose):
    seg = tt // SUBLANES
    r = lax.broadcasted_iota(jnp.int32, (n, n), 1 if transpose else 0)
    c = lax.broadcasted_iota(jnp.int32, (n, n), 0 if transpose else 1)
    loc = r % tt
    src = (r - loc) + (loc % SUBLANES) * seg + loc // SUBLANES
    return jnp.where(c == src, 1.0, 0.0).astype(BF16)


def _rglru_kernel(x_ref, h0_ref, c0_ref, g_ref, win_ref, cw_ref, cb_ref, wax_ref,
                  ba_ref, bx_ref, lam_ref, wout_ref,
                  o_ref, h_ref, c_ref,
                  xb_scr, hs_scr, to_seg_scr, from_seg_scr,
                  *, bb, gs, tt, d_rnn, shared_init):
    t = pl.program_id(1)
    gn = gs * tt
    bw = d_rnn // RG_BLOCKS
    tail = CONV_W - 1
    seg = tt // SUBLANES

    @pl.when(t == 0)
    def _():
        for s in range(bb):
            h_ref[s] = h0_ref[0 if shared_init else s]
            c_ref[s] = c0_ref[0 if shared_init else s]
        to_seg_scr[...] = _segment_perm(gn, tt, False)
        from_seg_scr[...] = _segment_perm(gn, tt, True)

    cw = cw_ref[...]
    softplus_neg_lam = jax.nn.softplus(-lam_ref[...])
    sub = lax.broadcasted_iota(jnp.int32, (SUBLANES, d_rnn), 0)

    def stream_pieces(s):
        base = s * tt
        xg = [xb_scr[pl.ds(pl.multiple_of(base + g * SUBLANES, SUBLANES), SUBLANES), :] for g in range(seg)]
        prev = c_ref[s]
        h0 = h_ref[s]

        def delayed(g, j):
            gg, wraps = g - j, 0
            while gg < 0:
                gg, wraps = gg + seg, wraps + 1
            v = xg[gg]
            if wraps:
                v = pltpu.roll(v, wraps, 0)
                for sl in range(wraps):
                    i = tail + sl * seg + g - j
                    v = jnp.where(sub == sl, prev[i:i + 1, :], v)
            return v

        conv = []
        for g in range(seg):
            acc = cb_ref[...] + delayed(g, tail) * cw[0:1, :]
            for j in range(1, CONV_W):
                acc = acc + delayed(g, tail - j) * cw[j:j + 1, :]
            conv.append(acc)
        for i in range(tail):
            step = tt - tail + i
            c_ref[s, i:i + 1, :] = xg[step % seg][step // seg:step // seg + 1, :]
        conv = jnp.concatenate(conv, axis=0)
        yield

        cfb = conv.astype(BF16)
        pre = [_dot(cfb[:, i * bw:(i + 1) * bw], wax_ref[i]) for i in range(RG_BLOCKS)]
        r = _sigmoid(jnp.concatenate([p[:, :bw] for p in pre], axis=-1) + ba_ref[...])
        ig = _sigmoid(jnp.concatenate([p[:, bw:] for p in pre], axis=-1) + bx_ref[...])
        yield
        log_a = -RG_C * r * softplus_neg_lam
        a = jnp.exp(log_a)
        u = jnp.sqrt(-jnp.tanh(log_a) * (a * a + 1.0)) * (ig * conv)
        yield

        hz = [u[0:SUBLANES]]
        az = [a[0:SUBLANES]]
        for g in range(1, seg):
            ag = a[g * SUBLANES:(g + 1) * SUBLANES]
            hz.append(ag * hz[-1] + u[g * SUBLANES:(g + 1) * SUBLANES])
            az.append(ag * az[-1])
        e_end, a_end = hz[-1], az[-1]
        d = 1
        while d < SUBLANES:
            keep = sub >= d
            e_sh = pltpu.roll(e_end, d, 0)
            a_sh = pltpu.roll(a_end, d, 0)
            e_end = jnp.where(keep, a_end * e_sh + e_end, e_end)
            a_end = jnp.where(keep, a_end * a_sh, a_end)
            d *= 2
        h_end = a_end * h0 + e_end
        h_start = jnp.where(sub == 0, h0, pltpu.roll(h_end, 1, 0))
        h_ref[s] = h_end[SUBLANES - 1:SUBLANES, :]
        for g in range(seg):
            rows = pl.ds(pl.multiple_of(base + g * SUBLANES, SUBLANES), SUBLANES)
            hs_scr[rows, :] = hz[g] + az[g] * h_start

    def chain(grp):
        r0 = grp * gn
        x = x_ref[grp * gs:(grp + 1) * gs].reshape(gn, x_ref.shape[-1])
        hn = _rms(x, g_ref[...]).astype(BF16)
        hp = _dot(to_seg_scr[...], hn).astype(BF16)
        yield
        xb_scr[r0:r0 + gn, :] = _dot(hp, win_ref[:, :d_rnn])
        yield
        gb = _dot(hp, win_ref[:, d_rnn:])
        yield

        if gs == 1:
            yield from stream_pieces(grp)
        else:
            _for_each(gs, lambda i: list(stream_pieces(grp * gs + i)))
        yield

        y = (hs_scr[r0:r0 + gn, :] * jax.nn.gelu(gb)).astype(BF16)
        yield
        yn = _dot(from_seg_scr[...], y).astype(BF16)
        o_ref[grp * gs:(grp + 1) * gs] = (x + _dot(yn, wout_ref[...])).reshape(gs, tt, x.shape[-1])

    _run_interleaved([chain(grp) for grp in range(bb // gs)])


def _rglru(x, h0, c0, shared_init, g, w_in, conv_w, conv_b, wax, ba, bx, lam, w_out, lead, *, bb, gs, tt):
    bsz, tlen, d = x.shape
    d_rnn = h0.shape[-1]
    tail = c0.shape[1]
    n = bb * tt
    shared = shared_init is not None
    kern = functools.partial(_rglru_kernel, bb=bb, gs=gs, tt=tt, d_rnn=d_rnn, shared_init=shared)
    row = lambda v: v.reshape(1, d_rnn)
    if shared:
        init_map, init_n = (lambda b, t: (shared_init, 0, 0)), 1
    else:
        init_map, init_n = (lambda b, t: (b, 0, 0)), bb
    return pl.pallas_call(
        kern,
        grid=(bsz // bb, tlen // tt),
        in_specs=[
            pl.BlockSpec((bb, tt, d), lambda b, t: (b, t, 0)),
            pl.BlockSpec((init_n, 1, d_rnn), init_map),
            pl.BlockSpec((init_n, tail, d_rnn), init_map),
            _const_spec((1, d)),
            _pick_spec(w_in.shape, lead),
            _const_spec(conv_w.shape),
            _const_spec((1, d_rnn)),
            _pick_spec(wax.shape, lead),
            _const_spec((1, d_rnn)),
            _const_spec((1, d_rnn)),
            _const_spec((1, d_rnn)),
            _pick_spec(w_out.shape, lead),
        ],
        out_specs=[
            pl.BlockSpec((bb, tt, d), lambda b, t: (b, t, 0)),
            pl.BlockSpec((bb, 1, d_rnn), lambda b, t: (b, 0, 0)),
            pl.BlockSpec((bb, tail, d_rnn), lambda b, t: (b, 0, 0)),
        ],
        out_shape=[
            jax.ShapeDtypeStruct((bsz, tlen, d), F32),
            jax.ShapeDtypeStruct((bsz, 1, d_rnn), F32),
            jax.ShapeDtypeStruct((bsz, tail, d_rnn), F32),
        ],
        scratch_shapes=[
            pltpu.VMEM((n, d_rnn), F32),
            pltpu.VMEM((n, d_rnn), F32),
            pltpu.VMEM((gs * tt, gs * tt), BF16),
            pltpu.VMEM((gs * tt, gs * tt), BF16),
        ],
        compiler_params=pltpu.CompilerParams(
            dimension_semantics=("arbitrary", "arbitrary"), vmem_limit_bytes=VMEM_LIMIT_BYTES),
        name="rglru",
    )(x, h0, c0, g.reshape(1, d), w_in, conv_w, row(conv_b), wax, row(ba), row(bx), row(lam), w_out)


def _trunk(x, s_hgrn, s_h, s_conv, shared_init, w, ffn, *, hgrn_tiles, rglru_tiles):
    bsz, tlen, d = x.shape
    depth = w["mix_norm"].shape[0]
    new_s, new_h, new_c = [], [], []
    h = x.reshape(bsz * tlen, d)
    for layer in range(depth):
        j = layer // N_MIXERS
        h = ffn(h, layer, 0, False)
        h3 = h.reshape(bsz, tlen, d)
        if layer % N_MIXERS == 0:
            h3, s_new = _hgrn(h3, s_hgrn[j], shared_init, w["mix_norm"][layer], w["a_w_in"], w["a_lb"],
                              w["a_onorm"][j], w["a_w_out"], (j,), layer_j=j, **hgrn_tiles)
            new_s.append(s_new)
        else:
            h3, h_new, c_new = _rglru(h3, s_h[j], s_conv[j], shared_init, w["mix_norm"][layer], w["b_w_in"],
                                      w["b_conv_w"][j], w["b_conv_b"][j], w["b_wax"], w["b_ba"][j],
                                      w["b_bx"][j], w["b_lambda"][j], w["b_w_out"], (j,), **rglru_tiles)
            new_h.append(h_new)
            new_c.append(c_new)
        h = h3.reshape(bsz * tlen, d)
        h = ffn(h, layer, 1, layer == depth - 1)
    return h.reshape(bsz, tlen, d), new_s, new_h, new_c


def kernel(x_prompt, x_sample, state_hgrn, state_rglru, state_conv, meta_tokens, ffn_norm, ffn_w_in, ffn_w_out, mix_norm, a_w_in, a_lb, a_onorm, a_w_out, b_w_in, b_conv_w, b_conv_b, b_wa, b_ba, b_wx, b_bx, b_lambda, b_w_out, final_norm):
    n_dec, dec_seq, d = x_sample.shape
    n_meta = meta_tokens.shape[0]
    assert n_meta == dec_seq, "the meta prefix is run as one more stream of the short pass"

    w = dict(
        mix_norm=mix_norm,
        a_w_in=a_w_in.astype(BF16), a_lb=a_lb, a_onorm=a_onorm, a_w_out=a_w_out.astype(BF16),
        b_w_in=b_w_in.astype(BF16), b_conv_w=b_conv_w, b_conv_b=b_conv_b,
        b_wax=jnp.concatenate([b_wa, b_wx], axis=-1).astype(BF16),
        b_ba=b_ba, b_bx=b_bx, b_lambda=b_lambda, b_w_out=b_w_out.astype(BF16),
    )

    def with_zero_stream(s):
        return jnp.concatenate([s, jnp.zeros_like(s[:, :1])], axis=1)

    xs = jnp.concatenate([x_sample, meta_tokens.astype(x_sample.dtype)[None]], axis=0)
    n_short = n_dec + 1

    ffn_bf16 = {}

    def ffn_short(h, layer, slot, last):
        h, *ffn_bf16[layer, slot] = _ffn_cast(
            h, ffn_norm[layer, slot], ffn_w_in, ffn_w_out, (layer, slot), final_norm,
            final_norm=last)
        return h

    def ffn_long(h, layer, slot, last):
        return _ffn(h, ffn_norm[layer, slot], *ffn_bf16[layer, slot], final_norm,
                    tm=1024, parts=2, final_norm=last)

    ys, s_s, h_s, c_s = _trunk(
        xs, with_zero_stream(state_hgrn), with_zero_stream(state_rglru)[:, :, None, :],
        with_zero_stream(state_conv), None, w, ffn_short,
        hgrn_tiles=dict(bb=n_short, gs=n_short, tt=dec_seq, chunk=dec_seq, pair=1, keys_t=False),
        rglru_tiles=dict(bb=n_short, gs=n_short, tt=dec_seq))

    yp, s_p, h_p, c_p = _trunk(
        x_prompt, s_s, h_s, c_s, n_dec, w, ffn_long,
        hgrn_tiles=dict(bb=2, gs=1, tt=256, chunk=32, pair=2, keys_t=True),
        rglru_tiles=dict(bb=4, gs=1, tt=256))

    return (yp, ys[:n_dec],
            jnp.stack(s_p), jnp.stack([s[:n_dec] for s in s_s]),
            jnp.stack([h[:, 0] for h in h_p]), jnp.stack([h[:n_dec, 0] for h in h_s]),
            jnp.stack(c_p), jnp.stack([c[:n_dec] for c in c_s]))
```
